```python
import math
import jax, jax.numpy as jnp
from jax import lax
import numpy as np

D_MODEL = 2048
BATCH = 4
SEQ = 8192
DEPTH = 1

GRID_W = 64
CTX_LEN = 256
MIX_WIDTH = D_MODEL
DA_HEADS = 8
DA_QK_DIM = 64
DA_V_DIM = 2 * DA_QK_DIM
DA_WIDTH = DA_HEADS * DA_V_DIM
NA_HEADS = 8
NA_HEAD_DIM = 128
NA_WIDTH = NA_HEADS * NA_HEAD_DIM
NA_WIN_ROWS = 8
NA_WIN_COLS = 16
Q_BLOCK = 128
ROPE_BASE = 10000.0
NORM_EPS = 1e-6
SUBLN_EPS = 1e-5
IN_COLS = 4 * DA_WIDTH + 4 * NA_WIDTH
NA_OFF = 4 * DA_WIDTH

kernel_name = "hybrid_diffattn_natten_dit_layer"


def rmsnorm(x, g, eps=NORM_EPS):
    xf = x.astype(jnp.float32)
    y = xf * lax.rsqrt(jnp.mean(xf * xf, axis=-1, keepdims=True) + eps)
    return (y * g.astype(jnp.float32)).astype(x.dtype)


def rope_1d(x, pos):
    half = x.shape[-1] // 2
    inv_freq = ROPE_BASE ** (-jnp.arange(half, dtype=jnp.float32) / half)
    ang = pos.astype(jnp.float32)[:, None] * inv_freq[None, :]
    cos = jnp.cos(ang).astype(x.dtype)
    sin = jnp.sin(ang).astype(x.dtype)
    x1, x2 = x[..., :half], x[..., half:]
    return jnp.concatenate([x1 * cos - x2 * sin, x2 * cos + x1 * sin], axis=-1)


def axial_rope(x, row_pos, col_pos):
    a = x.shape[-1] // 2
    return jnp.concatenate([rope_1d(x[..., :a], row_pos), rope_1d(x[..., a:], col_pos)], axis=-1)


def diff_attention(q, k, v, k_ctx, v_ctx, row_pos, col_pos, lam, subln_g, lambda_init):
    B, S, _ = q.shape
    n_ctx = k_ctx.shape[1]
    split_qk = lambda t, n: t.reshape(B, n, DA_HEADS, 2, DA_QK_DIM).transpose(0, 2, 3, 1, 4)
    split_v = lambda t, n: t.reshape(B, n, DA_HEADS, DA_V_DIM).transpose(0, 2, 1, 3)
    qh = axial_rope(split_qk(q, S), row_pos, col_pos) * (DA_QK_DIM ** -0.5)
    kh = axial_rope(split_qk(k, S), row_pos, col_pos)
    k_all = jnp.concatenate([kh, split_qk(k_ctx, n_ctx)], axis=3)
    v_all = jnp.concatenate([split_v(v, S), split_v(v_ctx, n_ctx)], axis=2)
    nb = S // Q_BLOCK
    q_blocks = jnp.moveaxis(qh.reshape(B, DA_HEADS, 2, nb, Q_BLOCK, DA_QK_DIM), 3, 0)

    def block(qb):
        s = jnp.einsum('bhiqd,bhikd->bhiqk', qb, k_all).astype(jnp.float32)
        p = jax.nn.softmax(s, axis=-1)
        attn = p[:, :, 0] - lam * p[:, :, 1]
        return jnp.einsum('bhqk,bhkd->bhqd', attn.astype(v_all.dtype), v_all)

    out = lax.map(block, q_blocks)
    out = out.transpose(1, 0, 3, 2, 4).reshape(B, S, DA_HEADS, DA_V_DIM)
    out = rmsnorm(out, subln_g, SUBLN_EPS) * (1.0 - lambda_init)
    return out.reshape(B, S, DA_WIDTH)


def neighbourhood_attention(q, k, v, k_ctx, v_ctx, rpb_l, rows):
    B, S, _ = q.shape
    n_ctx = k_ctx.shape[1]
    kr = min(NA_WIN_ROWS, rows)
    kc = min(NA_WIN_COLS, GRID_W)
    grid = lambda t: t.reshape(B, rows, GRID_W, NA_HEADS, NA_HEAD_DIM).transpose(0, 3, 1, 2, 4)
    qg = grid(q) * (NA_HEAD_DIM ** -0.5)
    kg = grid(k)
    vg = grid(v)
    kcx = k_ctx.reshape(B, n_ctx, NA_HEADS, NA_HEAD_DIM).transpose(0, 2, 1, 3)
    vcx = v_ctx.reshape(B, n_ctx, NA_HEADS, NA_HEAD_DIM).transpose(0, 2, 1, 3)
    j = np.arange(GRID_W)
    c0 = np.clip(j - kc // 2, 0, GRID_W - kc)
    col_idx_np = c0[:, None] + np.arange(kc)[None, :]
    col_off_np = col_idx_np - j[:, None] + (NA_WIN_COLS - 1)
    col_idx = jnp.asarray(col_idx_np, dtype=jnp.int32)
    bias_cols = rpb_l[:, :, jnp.asarray(col_off_np, dtype=jnp.int32)]
    n_win = kr * kc

    def row_block(args):
        r, q_row = args
        r0 = jnp.clip(r - kr // 2, 0, rows - kr)
        k_band = lax.dynamic_slice_in_dim(kg, r0, kr, axis=2)
        v_band = lax.dynamic_slice_in_dim(vg, r0, kr, axis=2)
        k_win = k_band[:, :, :, col_idx]
        v_win = v_band[:, :, :, col_idx]
        row_off = r0 + jnp.arange(kr, dtype=jnp.int32) - r + (NA_WIN_ROWS - 1)
        bias = jnp.transpose(bias_cols[:, row_off], (0, 2, 1, 3))
        s_win = jnp.einsum('bhqd,bhrqcd->bhqrc', q_row, k_win).astype(jnp.float32) + bias.astype(jnp.float32)
        s_ctx = jnp.einsum('bhqd,bhcd->bhqc', q_row, kcx).astype(jnp.float32)
        s = jnp.concatenate([s_win.reshape(B, NA_HEADS, GRID_W, n_win), s_ctx], axis=-1)
        p = jax.nn.softmax(s, axis=-1).astype(v.dtype)
        p_win = p[..., :n_win].reshape(B, NA_HEADS, GRID_W, kr, kc)
        return (jnp.einsum('bhqrc,bhrqcd->bhqd', p_win, v_win)
                + jnp.einsum('bhqc,bhcd->bhqd', p[..., n_win:], vcx))

    out = lax.map(row_block, (jnp.arange(rows, dtype=jnp.int32), jnp.moveaxis(qg, 2, 0)))
    return out.transpose(1, 0, 3, 2, 4).reshape(B, S, NA_WIDTH)


def setup_inputs(seed: int = 0) -> dict:
    key = jax.random.key(seed)
    ks = jax.random.split(key, 16)
    f32 = jnp.float32
    nrm = lambda k, shape, s: jax.random.normal(k, shape, f32) * s
    return {
        "x": nrm(ks[0], (BATCH, SEQ, D_MODEL), 1.0),
        "c": nrm(ks[1], (BATCH, D_MODEL), 1.0),
        "ctx": nrm(ks[2], (BATCH, CTX_LEN, D_MODEL), 1.0),
        "c_ctx": nrm(ks[3], (D_MODEL,), 1.0),
        "norm_g": 1.0 + nrm(ks[4], (DEPTH, D_MODEL), 0.02),
        "w_mod": nrm(ks[5], (DEPTH, D_MODEL, 3 * D_MODEL), 0.5 * D_MODEL ** -0.5),
        "b_mod": nrm(ks[6], (DEPTH, 3 * D_MODEL), 0.02),
        "w_in": nrm(ks[7], (DEPTH, D_MODEL, IN_COLS), D_MODEL ** -0.5),
        "w_out": nrm(ks[8], (DEPTH, MIX_WIDTH, D_MODEL), MIX_WIDTH ** -0.5),
        "lam_q1": nrm(ks[9], (DEPTH, DA_QK_DIM), 0.1),
        "lam_k1": nrm(ks[10], (DEPTH, DA_QK_DIM), 0.1),
        "lam_q2": nrm(ks[11], (DEPTH, DA_QK_DIM), 0.1),
        "lam_k2": nrm(ks[12], (DEPTH, DA_QK_DIM), 0.1),
        "subln_g": 1.0 + nrm(ks[13], (DEPTH, DA_V_DIM), 0.02),
        "rpb": nrm(ks[14], (DEPTH, NA_HEADS, 2 * NA_WIN_ROWS - 1, 2 * NA_WIN_COLS - 1), 0.1),
        "final_g": 1.0 + nrm(ks[15], (D_MODEL,), 0.02),
    }


def reference(x, c, ctx, c_ctx, norm_g, w_mod, b_mod, w_in, w_out, lam_q1, lam_k1, lam_q2, lam_k2, subln_g, rpb, final_g):
    B, S, _ = x.shape
    rows = S // GRID_W
    t = jnp.arange(S, dtype=jnp.int32)
    row_pos = t // GRID_W
    col_pos = t % GRID_W
    h = x
    for layer in range(DEPTH):
        lambda_init = 0.8 - 0.6 * math.exp(-0.3 * layer)
        wm, bm, wi = w_mod[layer], b_mod[layer], w_in[layer]
        shift, scale, gate = jnp.split(jax.nn.silu(c) @ wm + bm, 3, axis=-1)
        shift_c, scale_c = jnp.split(jax.nn.silu(c_ctx) @ wm[:, :2 * D_MODEL] + bm[:2 * D_MODEL], 2, axis=-1)
        hx = rmsnorm(h, norm_g[layer]) * (1.0 + scale[:, None]) + shift[:, None]
        hc = rmsnorm(ctx, norm_g[layer]) * (1.0 + scale_c) + shift_c
        proj = hx @ wi
        qa, ka, va, ga, qb, kb, vb, gb = jnp.split(proj, 8, axis=-1)
        ka_c, va_c = jnp.split(hc @ wi[:, DA_WIDTH:3 * DA_WIDTH], 2, axis=-1)
        kb_c, vb_c = jnp.split(hc @ wi[:, NA_OFF + NA_WIDTH:NA_OFF + 3 * NA_WIDTH], 2, axis=-1)
        lq1, lk1 = lam_q1[layer].astype(jnp.float32), lam_k1[layer].astype(jnp.float32)
        lq2, lk2 = lam_q2[layer].astype(jnp.float32), lam_k2[layer].astype(jnp.float32)
        lam = jnp.exp(jnp.sum(lq1 * lk1)) - jnp.exp(jnp.sum(lq2 * lk2)) + lambda_init
        oa = diff_attention(qa, ka, va, ka_c, va_c, row_pos, col_pos, lam, subln_g[layer], lambda_init)
        oa = oa * jax.nn.silu(ga)
        ob = neighbourhood_attention(qb, kb, vb, kb_c, vb_c, rpb[layer], rows) * jax.nn.silu(gb)
        mixed = jnp.concatenate([oa, ob], axis=-1) @ w_out[layer]
        h = h + gate[:, None] * mixed
    return rmsnorm(h, final_g)
```

```python
import functools
import math

import numpy as np
import jax
import jax.numpy as jnp
from jax import lax
from jax.experimental import pallas as pl
from jax.experimental.pallas import tpu as pltpu

GRID_W = 64
DA_HEADS = 8
DA_QK_DIM = 64
DA_V_DIM = 2 * DA_QK_DIM
NA_HEADS = 8
NA_HEAD_DIM = 128
NA_WIN_ROWS = 8
NA_WIN_COLS = 16
ROPE_BASE = 10000.0
NORM_EPS = 1e-6
SUBLN_EPS = 1e-5

LANES = 128
VMEM_LIMIT_BYTES = 56 * 1024 * 1024
LOG2E = math.log2(math.e)
MASK_VALUE = -1e30

COL_QA, COL_KA, COL_VA, COL_GA, COL_QB, COL_KB, COL_VB, COL_GB = range(8)


def _silu(x):
    return x * (1.0 / (1.0 + jnp.exp(-x)))


def _mod_kernel(c_ref, w_ref, b_ref, o_ref):
    a = _silu(c_ref[...]).astype(jnp.bfloat16)
    w = w_ref[...].astype(jnp.bfloat16)
    o_ref[...] = jnp.dot(a, w, preferred_element_type=jnp.float32) + b_ref[...]


def _modulation(cc, w_mod, b_mod):
    rows, d = cc.shape
    n = w_mod.shape[1]
    tn = 768
    return pl.pallas_call(
        _mod_kernel,
        grid=(n // tn,),
        in_specs=[
            pl.BlockSpec((rows, d), lambda j: (0, 0)),
            pl.BlockSpec((d, tn), lambda j: (0, j)),
            pl.BlockSpec((1, tn), lambda j: (0, j)),
        ],
        out_specs=pl.BlockSpec((rows, tn), lambda j: (0, j)),
        out_shape=jax.ShapeDtypeStruct((rows, n), jnp.float32),
        compiler_params=pltpu.CompilerParams(
            dimension_semantics=("parallel",), vmem_limit_bytes=VMEM_LIMIT_BYTES),
        name="modulation",
    )(cc, w_mod, b_mod.reshape(1, n))


def _rope_tables(seq):
    t = jnp.arange(seq, dtype=jnp.int32)
    row_pos = (t // GRID_W).astype(jnp.float32)
    col_pos = (t % GRID_W).astype(jnp.float32)
    half = DA_QK_DIM // 4
    inv_freq = ROPE_BASE ** (-jnp.arange(half, dtype=jnp.float32) / half)
    lane = np.arange(LANES)
    d = lane % DA_QK_DIM
    use_col = (d // (DA_QK_DIM // 2)) == 1
    f = d % half
    first_half = (d % (2 * half)) < half
    pos = jnp.where(jnp.asarray(use_col)[None, :], col_pos[:, None], row_pos[:, None])
    ang = pos * inv_freq[jnp.asarray(f)][None, :]
    cos = jnp.cos(ang)
    sin = jnp.sin(ang)
    sin_signed = jnp.where(jnp.asarray(first_half)[None, :], -sin, sin)
    return cos, sin_signed


def _proj_kernel(*refs, kinds, q_scales, with_rope):
    if with_rope:
        x_ref, g_ref, scale_ref, shift_ref, w_ref, cos_ref, sin_ref, o_ref, hx_ref = refs
    else:
        x_ref, g_ref, scale_ref, shift_ref, w_ref, o_ref, hx_ref = refs
    j = pl.program_id(1)

    @pl.when(j == 0)
    def _():
        x = x_ref[...]
        ms = jnp.mean(x * x, axis=-1, keepdims=True)
        y = x * lax.rsqrt(ms + NORM_EPS) * g_ref[...]
        hx = y * (1.0 + scale_ref[0]) + shift_ref[0]
        hx_ref[...] = hx.astype(jnp.bfloat16)

    acc = jnp.dot(hx_ref[...], w_ref[...], preferred_element_type=jnp.float32)
    tn = acc.shape[1]

    def plain():
        o_ref[...] = acc.astype(o_ref.dtype)

    def silu():
        o_ref[...] = _silu(acc).astype(o_ref.dtype)

    def scaled(s):
        o_ref[...] = (acc * s).astype(o_ref.dtype)

    def rope(s):
        cos = cos_ref[...]
        sin = sin_ref[...]
        lane = lax.broadcasted_iota(jnp.int32, cos.shape, 1)
        half = DA_QK_DIM // 4
        first_half = (lane % (2 * half)) < half
        for hh in range(tn // LANES):
            xs = acc[:, hh * LANES:(hh + 1) * LANES]
            fwd = pltpu.roll(xs, LANES - half, 1)
            bwd = pltpu.roll(xs, half, 1)
            y = xs * cos + jnp.where(first_half, fwd, bwd) * sin
            if s != 1.0:
                y = y * s
            o_ref[:, hh * LANES:(hh + 1) * LANES] = y.astype(o_ref.dtype)

    emit = {
        "plain": plain,
        "silu": silu,
        "scale_qb": functools.partial(scaled, q_scales[1]),
        "rope_q": functools.partial(rope, q_scales[0]),
        "rope_k": functools.partial(rope, 1.0),
    }
    distinct = sorted(set(kinds))
    if len(distinct) == 1:
        emit[distinct[0]]()
    else:
        for kind in distinct:
            cond = functools.reduce(
                jnp.logical_or, [j == jj for jj, kk in enumerate(kinds) if kk == kind])
            pl.when(cond)(emit[kind])


def _in_projection(x2, norm_g, scale, shift, w_bf16, *, tm, tn, rows_per_mod, col_blocks,
                   kinds, q_scales, rope_tabs, name):
    m, d = x2.shape
    nj = len(col_blocks)
    with_rope = rope_tabs is not None
    blocks_per_mod = rows_per_mod // tm
    start, skip = col_blocks[0], 0
    if nj > 2:
        skip = col_blocks[2] - col_blocks[1] - 1
    assert all(col_blocks[jj] == start + jj + skip * (jj // 2) for jj in range(nj))

    def w_map(i, j):
        return (0, start + j + skip * (j // 2))

    in_specs = [
        pl.BlockSpec((tm, d), lambda i, j: (i, 0)),
        pl.BlockSpec((1, d), lambda i, j: (0, 0)),
        pl.BlockSpec((1, 1, d), lambda i, j: (i // blocks_per_mod, 0, 0)),
        pl.BlockSpec((1, 1, d), lambda i, j: (i // blocks_per_mod, 0, 0)),
        pl.BlockSpec((d, tn), w_map),
    ]
    args = [x2, norm_g.reshape(1, d), scale, shift, w_bf16]
    if with_rope:
        seq_blocks = rope_tabs[0].shape[0] // tm
        in_specs += [pl.BlockSpec((tm, LANES), lambda i, j: (i % seq_blocks, 0))] * 2
        args += list(rope_tabs)
    kernel = functools.partial(_proj_kernel, kinds=tuple(kinds), q_scales=q_scales,
                               with_rope=with_rope)
    return pl.pallas_call(
        kernel,
        grid=(m // tm, nj),
        in_specs=in_specs,
        out_specs=pl.BlockSpec((tm, tn), lambda i, j: (i, j)),
        out_shape=jax.ShapeDtypeStruct((m, nj * tn), jnp.bfloat16),
        scratch_shapes=[pltpu.VMEM((tm, d), jnp.bfloat16)],
        compiler_params=pltpu.CompilerParams(
            dimension_semantics=("parallel", "arbitrary"), vmem_limit_bytes=VMEM_LIMIT_BYTES),
        name=name,
    )(*args)


def _diff_attn_kernel(q_ref, k_ref, v_ref, kc_ref, vc_ref, sg_ref, lam_ref, subg_ref, o_ref,
                      m_ref, l_ref, acc_ref, *, tkb, lambda_init):
    tq = q_ref.shape[1]
    seq = k_ref.shape[1]
    q = q_ref[0]
    lane = lax.broadcasted_iota(jnp.int32, q.shape, 1)
    zero = jnp.zeros_like(q)
    qq = jnp.concatenate([jnp.where(lane < DA_QK_DIM, q, zero),
                          jnp.where(lane >= DA_QK_DIM, q, zero)], axis=0)

    m_ref[...] = jnp.full(m_ref.shape, -jnp.inf, jnp.float32)
    l_ref[...] = jnp.zeros(l_ref.shape, jnp.float32)
    acc_ref[...] = jnp.zeros(acc_ref.shape, jnp.float32)

    def online_step(kblk, vblk):
        s = lax.dot_general(qq, kblk, (((1,), (1,)), ((), ())),
                            preferred_element_type=jnp.float32)
        m_prev = m_ref[...]
        m_new = jnp.maximum(m_prev, jnp.max(s, axis=1, keepdims=True))
        alpha = jnp.exp2(m_prev - m_new)
        p = jnp.exp2(s - m_new)
        l_ref[...] = alpha * l_ref[...] + jnp.sum(p, axis=1, keepdims=True)
        acc_ref[...] = alpha * acc_ref[...] + jnp.dot(
            p.astype(jnp.bfloat16), vblk, preferred_element_type=jnp.float32)
        m_ref[...] = m_new

    def body(jb, carry):
        off = pl.multiple_of(jb * tkb, tkb)
        online_step(k_ref[0, pl.ds(off, tkb), :], v_ref[0, pl.ds(off, tkb), :])
        return carry

    lax.fori_loop(0, seq // tkb, body, 0)
    online_step(kc_ref[0], vc_ref[0])

    lam_p = lam_ref[...]
    lam = (jnp.exp(jnp.sum(lam_p[0:1] * lam_p[1:2], axis=1, keepdims=True))
           - jnp.exp(jnp.sum(lam_p[2:3] * lam_p[3:4], axis=1, keepdims=True)) + lambda_init)
    o = acc_ref[...] * (1.0 / l_ref[...])
    attn = o[:tq] - lam * o[tq:]
    ms = jnp.mean(attn * attn, axis=-1, keepdims=True)
    y = attn * lax.rsqrt(ms + SUBLN_EPS) * subg_ref[...]
    y = y * (1.0 - lambda_init)
    o_ref[0] = (y * sg_ref[0].astype(jnp.float32)).astype(o_ref.dtype)


def _diff_attention(proj, ctxp, lam_p, subln_g, *, lambda_init, tq, tkb):
    b, seq, _ = proj.shape
    n_ctx = ctxp.shape[1]
    hb = DA_HEADS
    kernel = functools.partial(_diff_attn_kernel, tkb=tkb, lambda_init=lambda_init)
    return pl.pallas_call(
        kernel,
        grid=(b, DA_HEADS, seq // tq),
        in_specs=[
            pl.BlockSpec((1, tq, LANES), lambda bi, h, qi: (bi, qi, COL_QA * hb + h)),
            pl.BlockSpec((1, seq, LANES), lambda bi, h, qi: (bi, 0, COL_KA * hb + h)),
            pl.BlockSpec((1, seq, LANES), lambda bi, h, qi: (bi, 0, COL_VA * hb + h)),
            pl.BlockSpec((1, n_ctx, LANES), lambda bi, h, qi: (bi, 0, 0 * hb + h)),
            pl.BlockSpec((1, n_ctx, LANES), lambda bi, h, qi: (bi, 0, 1 * hb + h)),
            pl.BlockSpec((1, tq, LANES), lambda bi, h, qi: (bi, qi, COL_GA * hb + h)),
            pl.BlockSpec(lam_p.shape, lambda bi, h, qi: (0, 0)),
            pl.BlockSpec((1, LANES), lambda bi, h, qi: (0, 0)),
        ],
        out_specs=pl.BlockSpec((1, tq, LANES), lambda bi, h, qi: (bi, qi, h)),
        out_shape=jax.ShapeDtypeStruct((b, seq, DA_HEADS * DA_V_DIM), jnp.bfloat16),
        scratch_shapes=[
            pltpu.VMEM((2 * tq, 1), jnp.float32),
            pltpu.VMEM((2 * tq, 1), jnp.float32),
            pltpu.VMEM((2 * tq, DA_V_DIM), jnp.float32),
        ],
        compiler_params=pltpu.CompilerParams(
            dimension_semantics=("parallel", "parallel", "arbitrary"),
            vmem_limit_bytes=VMEM_LIMIT_BYTES),
        name="diff_attention",
    )(proj, proj, proj, ctxp, ctxp, proj, lam_p, subln_g.reshape(1, LANES))


NA_BLOCK_ROWS = NA_WIN_ROWS // 2
NA_BAND_ROWS = 3 * NA_BLOCK_ROWS


def _na_bias_table(rpb_l, rows):
    r_blk, r_band = NA_BLOCK_ROWS, NA_BAND_ROWS
    n_blocks = rows // r_blk
    row_off = np.zeros((3, r_blk, r_band), np.int32)
    row_ok = np.zeros((3, r_blk, r_band), bool)
    for variant, rb in enumerate((0, 1, n_blocks - 1)):
        band0 = min(max(rb - 1, 0), n_blocks - 3) * r_blk
        for qr in range(r_blk):
            r = rb * r_blk + qr
            r0 = min(max(r - NA_WIN_ROWS // 2, 0), rows - NA_WIN_ROWS)
            for kr in range(r_band):
                ka = band0 + kr
                row_ok[variant, qr, kr] = r0 <= ka < r0 + NA_WIN_ROWS
                row_off[variant, qr, kr] = np.clip(ka - r + NA_WIN_ROWS - 1, 0, 2 * NA_WIN_ROWS - 2)
    qc = np.arange(GRID_W)
    c0 = np.clip(qc - NA_WIN_COLS // 2, 0, GRID_W - NA_WIN_COLS)
    kc = np.arange(GRID_W)
    col_ok = (kc[None, :] >= c0[:, None]) & (kc[None, :] < c0[:, None] + NA_WIN_COLS)
    col_off = np.clip(kc[None, :] - qc[:, None] + NA_WIN_COLS - 1, 0, 2 * NA_WIN_COLS - 2)
    ro = jnp.asarray(row_off)[:, :, None, :, None]
    co = jnp.asarray(col_off)[None, None, :, None, :]
    ok = jnp.asarray(row_ok[:, :, None, :, None] & col_ok[None, None, :, None, :])
    vals = rpb_l.astype(jnp.float32)[:, ro, co] * LOG2E
    bias = jnp.where(ok[None], vals, MASK_VALUE)
    return bias.reshape(rpb_l.shape[0], 3, r_blk * GRID_W, r_band * GRID_W)


def _na_kernel(q_ref, k_ref, v_ref, kc_ref, vc_ref, sg_ref, bias_ref, o_ref):
    seq = q_ref.shape[1]
    tq = NA_BLOCK_ROWS * GRID_W
    band = NA_BAND_ROWS * GRID_W
    n_blocks = seq // tq
    kctx = kc_ref[0]
    vctx = vc_ref[0]
    dims = (((1,), (1,)), ((), ()))

    def body(rb, carry):
        q_off = pl.multiple_of(rb * tq, tq)
        k_off = pl.multiple_of(jnp.clip(rb - 1, 0, n_blocks - 3) * tq, tq)
        variant = jnp.where(rb == 0, 0, jnp.where(rb == n_blocks - 1, 2, 1))
        q = q_ref[0, pl.ds(q_off, tq), :]
        s_w = lax.dot_general(q, k_ref[0, pl.ds(k_off, band), :], dims,
                              preferred_element_type=jnp.float32) + bias_ref[0, variant]
        s_c = lax.dot_general(q, kctx, dims, preferred_element_type=jnp.float32)
        m = jnp.maximum(jnp.max(s_w, axis=1, keepdims=True), jnp.max(s_c, axis=1, keepdims=True))
        p_w = jnp.exp2(s_w - m)
        p_c = jnp.exp2(s_c - m)
        l = jnp.sum(p_w, axis=1, keepdims=True) + jnp.sum(p_c, axis=1, keepdims=True)
        o = (jnp.dot(p_w.astype(jnp.bfloat16), v_ref[0, pl.ds(k_off, band), :],
                     preferred_element_type=jnp.float32)
             + jnp.dot(p_c.astype(jnp.bfloat16), vctx, preferred_element_type=jnp.float32))
        o = o * (1.0 / l)
        o_ref[0, pl.ds(q_off, tq), :] = (
            o * sg_ref[0, pl.ds(q_off, tq), :].astype(jnp.float32)).astype(o_ref.dtype)
        return carry

    lax.fori_loop(0, n_blocks, body, 0)


def _neighbourhood_attention(proj, ctxp, bias):
    b, seq, _ = proj.shape
    n_ctx = ctxp.shape[1]
    hb = NA_HEADS
    tq = NA_BLOCK_ROWS * GRID_W
    band = NA_BAND_ROWS * GRID_W
    return pl.pallas_call(
        _na_kernel,
        grid=(b, NA_HEADS),
        in_specs=[
            pl.BlockSpec((1, seq, LANES), lambda bi, h: (bi, 0, COL_QB * hb + h)),
            pl.BlockSpec((1, seq, LANES), lambda bi, h: (bi, 0, COL_KB * hb + h)),
            pl.BlockSpec((1, seq, LANES), lambda bi, h: (bi, 0, COL_VB * hb + h)),
            pl.BlockSpec((1, n_ctx, LANES), lambda bi, h: (bi, 0, 2 * hb + h)),
            pl.BlockSpec((1, n_ctx, LANES), lambda bi, h: (bi, 0, 3 * hb + h)),
            pl.BlockSpec((1, seq, LANES), lambda bi, h: (bi, 0, COL_GB * hb + h)),
            pl.BlockSpec((1, 3, tq, band), lambda bi, h: (h, 0, 0, 0)),
        ],
        out_specs=pl.BlockSpec((1, seq, LANES), lambda bi, h: (bi, 0, h)),
        out_shape=jax.ShapeDtypeStruct((b, seq, NA_HEADS * NA_HEAD_DIM), jnp.bfloat16),
        compiler_params=pltpu.CompilerParams(
            dimension_semantics=("parallel", "parallel"), vmem_limit_bytes=VMEM_LIMIT_BYTES),
        name="neighbourhood_attention",
    )(proj, proj, proj, ctxp, ctxp, proj, bias)


def _out_kernel(oa_ref, ob_ref, wa_ref, wb_ref, x_ref, gate_ref, fg_ref, o_ref, *, final_norm):
    mixed = (jnp.dot(oa_ref[...], wa_ref[...], preferred_element_type=jnp.float32)
             + jnp.dot(ob_ref[...], wb_ref[...], preferred_element_type=jnp.float32))
    h = x_ref[...] + gate_ref[0] * mixed
    if final_norm:
        ms = jnp.mean(h * h, axis=-1, keepdims=True)
        h = h * lax.rsqrt(ms + NORM_EPS) * fg_ref[...]
    o_ref[...] = h


def _out_projection(oa2, ob2, w_out_bf16, x2, gate, final_g, *, tm, rows_per_mod, final_norm):
    m, d = x2.shape
    wa_rows = oa2.shape[1]
    wb_rows = ob2.shape[1]
    assert wa_rows == wb_rows
    blocks_per_mod = rows_per_mod // tm
    kernel = functools.partial(_out_kernel, final_norm=final_norm)
    return pl.pallas_call(
        kernel,
        grid=(m // tm,),
        in_specs=[
            pl.BlockSpec((tm, wa_rows), lambda i: (i, 0)),
            pl.BlockSpec((tm, wb_rows), lambda i: (i, 0)),
            pl.BlockSpec((wa_rows, d), lambda i: (0, 0)),
            pl.BlockSpec((wb_rows, d), lambda i: (1, 0)),
            pl.BlockSpec((tm, d), lambda i: (i, 0)),
            pl.BlockSpec((1, 1, d), lambda i: (i // blocks_per_mod, 0, 0)),
            pl.BlockSpec((1, d), lambda i: (0, 0)),
        ],
        out_specs=pl.BlockSpec((tm, d), lambda i: (i, 0)),
        out_shape=jax.ShapeDtypeStruct((m, d), jnp.float32),
        compiler_params=pltpu.CompilerParams(
            dimension_semantics=("parallel",), vmem_limit_bytes=VMEM_LIMIT_BYTES),
        name="out_projection",
    )(oa2, ob2, w_out_bf16, w_out_bf16, x2, gate, final_g.reshape(1, d))


def kernel(x, c, ctx, c_ctx, norm_g, w_mod, b_mod, w_in, w_out, lam_q1, lam_k1, lam_q2, lam_k2,
           subln_g, rpb, final_g):
    b, seq, d = x.shape
    n_ctx = ctx.shape[1]
    depth = w_in.shape[0]
    rows = seq // GRID_W
    group = DA_HEADS * DA_V_DIM
    assert w_in.shape[2] == 8 * group and NA_HEADS * NA_HEAD_DIM == group
    assert DA_V_DIM == LANES and NA_HEAD_DIM == LANES

    rope_tabs = _rope_tables(seq)
    q_scales = (DA_QK_DIM ** -0.5 * LOG2E, NA_HEAD_DIM ** -0.5 * LOG2E)
    x_kinds = ("rope_q", "rope_k", "plain", "silu", "scale_qb", "plain", "plain", "silu")
    ctx_cols = (COL_KA, COL_VA, COL_KB, COL_VB)

    mod_rows = 8
    cc = jnp.zeros((mod_rows, d), jnp.float32).at[:b].set(c).at[b].set(c_ctx)
    ctx2 = ctx.reshape(b * n_ctx, d)

    h2 = x.reshape(b * seq, d)
    for layer in range(depth):
        lambda_init = 0.8 - 0.6 * math.exp(-0.3 * layer)
        mod = _modulation(cc, w_mod[layer], b_mod[layer])
        shift = mod[:b, :d].reshape(b, 1, d)
        scale = mod[:b, d:2 * d].reshape(b, 1, d)
        gate = mod[:b, 2 * d:].reshape(b, 1, d)
        shift_c = mod[b:b + 1, :d].reshape(1, 1, d)
        scale_c = mod[b:b + 1, d:2 * d].reshape(1, 1, d)

        wi = w_in[layer].astype(jnp.bfloat16)
        proj = _in_projection(
            h2, norm_g[layer], scale, shift, wi, tm=1024, tn=group, rows_per_mod=seq,
            col_blocks=tuple(range(8)), kinds=x_kinds, q_scales=q_scales, rope_tabs=rope_tabs,
            name="in_projection").reshape(b, seq, 8 * group)
        ctxp = _in_projection(
            ctx2, norm_g[layer], scale_c, shift_c, wi, tm=b * n_ctx, tn=group,
            rows_per_mod=b * n_ctx, col_blocks=ctx_cols, kinds=("plain",) * 4, q_scales=q_scales,
            rope_tabs=None, name="ctx_projection").reshape(b, n_ctx, 4 * group)

        lam_p = jnp.stack([lam_q1[layer], lam_k1[layer], lam_q2[layer], lam_k2[layer]]
                          ).astype(jnp.float32)
        oa = _diff_attention(proj, ctxp, lam_p, subln_g[layer], lambda_init=lambda_init,
                             tq=512, tkb=1024)
        ob = _neighbourhood_attention(proj, ctxp, _na_bias_table(rpb[layer], rows))

        h2 = _out_projection(
            oa.reshape(b * seq, group), ob.reshape(b * seq, group),
            w_out[layer].astype(jnp.bfloat16), h2, gate, final_g, tm=512, rows_per_mod=seq,
            final_norm=(layer == depth - 1))
    return h2.reshape(b, seq, d)
```

```python
import functools
import math

import numpy as np
import jax
import jax.numpy as jnp
from jax import lax
from jax.experimental import pallas as pl
from jax.experimental.pallas import tpu as pltpu

GRID_W = 64
DA_HEADS = 8
DA_QK_DIM = 64
DA_V_DIM = 2 * DA_QK_DIM
NA_HEADS = 8
NA_HEAD_DIM = 128
NA_WIN_ROWS = 8
NA_WIN_COLS = 16
ROPE_BASE = 10000.0
NORM_EPS = 1e-6
SUBLN_EPS = 1e-5

LANES = 128
VMEM_LIMIT_BYTES = 56 * 1024 * 1024
LOG2E = math.log2(math.e)
MASK_VALUE = -1e30

COL_QA, COL_KA, COL_VA, COL_GA, COL_QB, COL_KB, COL_VB, COL_GB = range(8)


def _silu(x):
    return x * (1.0 / (1.0 + jnp.exp(-x)))


def _mod_kernel(c_ref, w_ref, b_ref, o_ref):
    a = _silu(c_ref[...]).astype(jnp.bfloat16)
    w = w_ref[...].astype(jnp.bfloat16)
    o_ref[...] = jnp.dot(a, w, preferred_element_type=jnp.float32) + b_ref[...]


def _modulation(cc, w_mod, b_mod):
    rows, d = cc.shape
    n = w_mod.shape[1]
    tn = 768
    return pl.pallas_call(
        _mod_kernel,
        grid=(n // tn,),
        in_specs=[
            pl.BlockSpec((rows, d), lambda j: (0, 0)),
            pl.BlockSpec((d, tn), lambda j: (0, j)),
            pl.BlockSpec((1, tn), lambda j: (0, j)),
        ],
        out_specs=pl.BlockSpec((rows, tn), lambda j: (0, j)),
        out_shape=jax.ShapeDtypeStruct((rows, n), jnp.float32),
        compiler_params=pltpu.CompilerParams(
            dimension_semantics=("parallel",), vmem_limit_bytes=VMEM_LIMIT_BYTES),
        name="modulation",
    )(cc, w_mod, b_mod.reshape(1, n))


def _rope_tables(seq):
    t = jnp.arange(seq, dtype=jnp.int32)
    row_pos = (t // GRID_W).astype(jnp.float32)
    col_pos = (t % GRID_W).astype(jnp.float32)
    half = DA_QK_DIM // 4
    inv_freq = ROPE_BASE ** (-jnp.arange(half, dtype=jnp.float32) / half)
    lane = np.arange(LANES)
    d = lane % DA_QK_DIM
    use_col = (d // (DA_QK_DIM // 2)) == 1
    f = d % half
    first_half = (d % (2 * half)) < half
    pos = jnp.where(jnp.asarray(use_col)[None, :], col_pos[:, None], row_pos[:, None])
    ang = pos * inv_freq[jnp.asarray(f)][None, :]
    cos = jnp.cos(ang)
    sin = jnp.sin(ang)
    sin_signed = jnp.where(jnp.asarray(first_half)[None, :], -sin, sin)
    return cos, sin_signed


def _proj_kernel(*refs, kinds, q_scales, with_rope):
    if with_rope:
        x_ref, g_ref, scale_ref, shift_ref, w_ref, cos_ref, sin_ref, o_ref, hx_ref = refs
    else:
        x_ref, g_ref, scale_ref, shift_ref, w_ref, o_ref, hx_ref = refs
    j = pl.program_id(1)

    @pl.when(j == 0)
    def _():
        x = x_ref[...]
        ms = jnp.mean(x * x, axis=-1, keepdims=True)
        y = x * lax.rsqrt(ms + NORM_EPS) * g_ref[...]
        hx = y * (1.0 + scale_ref[0]) + shift_ref[0]
        hx_ref[...] = hx.astype(jnp.bfloat16)

    acc = jnp.dot(hx_ref[...], w_ref[...], preferred_element_type=jnp.float32)
    tn = acc.shape[1]

    def plain():
        o_ref[...] = acc.astype(o_ref.dtype)

    def silu():
        o_ref[...] = _silu(acc).astype(o_ref.dtype)

    def scaled(s):
        o_ref[...] = (acc * s).astype(o_ref.dtype)

    def rope(s):
        cos = cos_ref[...]
        sin = sin_ref[...]
        lane = lax.broadcasted_iota(jnp.int32, cos.shape, 1)
        half = DA_QK_DIM // 4
        first_half = (lane % (2 * half)) < half
        for hh in range(tn // LANES):
            xs = acc[:, hh * LANES:(hh + 1) * LANES]
            fwd = pltpu.roll(xs, LANES - half, 1)
            bwd = pltpu.roll(xs, half, 1)
            y = xs * cos + jnp.where(first_half, fwd, bwd) * sin
            if s != 1.0:
                y = y * s
            o_ref[:, hh * LANES:(hh + 1) * LANES] = y.astype(o_ref.dtype)

    emit = {
        "plain": plain,
        "silu": silu,
        "scale_qb": functools.partial(scaled, q_scales[1]),
        "rope_q": functools.partial(rope, q_scales[0]),
        "rope_k": functools.partial(rope, 1.0),
    }
    distinct = sorted(set(kinds))
    if len(distinct) == 1:
        emit[distinct[0]]()
    else:
        for kind in distinct:
            cond = functools.reduce(
                jnp.logical_or, [j == jj for jj, kk in enumerate(kinds) if kk == kind])
            pl.when(cond)(emit[kind])


def _in_projection(x2, norm_g, scale, shift, w_bf16, *, tm, tn, rows_per_mod, col_blocks,
                   kinds, q_scales, rope_tabs, name):
    m, d = x2.shape
    nj = len(col_blocks)
    with_rope = rope_tabs is not None
    blocks_per_mod = rows_per_mod // tm
    start, skip = col_blocks[0], 0
    if nj > 2:
        skip = col_blocks[2] - col_blocks[1] - 1
    assert all(col_blocks[jj] == start + jj + skip * (jj // 2) for jj in range(nj))

    def w_map(i, j):
        return (0, start + j + skip * (j // 2))

    in_specs = [
        pl.BlockSpec((tm, d), lambda i, j: (i, 0)),
        pl.BlockSpec((1, d), lambda i, j: (0, 0)),
        pl.BlockSpec((1, 1, d), lambda i, j: (i // blocks_per_mod, 0, 0)),
        pl.BlockSpec((1, 1, d), lambda i, j: (i // blocks_per_mod, 0, 0)),
        pl.BlockSpec((d, tn), w_map),
    ]
    args = [x2, norm_g.reshape(1, d), scale, shift, w_bf16]
    if with_rope:
        seq_blocks = rope_tabs[0].shape[0] // tm
        in_specs += [pl.BlockSpec((tm, LANES), lambda i, j: (i % seq_blocks, 0))] * 2
        args += list(rope_tabs)
    kernel = functools.partial(_proj_kernel, kinds=tuple(kinds), q_scales=q_scales,
                               with_rope=with_rope)
    return pl.pallas_call(
        kernel,
        grid=(m // tm, nj),
        in_specs=in_specs,
        out_specs=pl.BlockSpec((tm, tn), lambda i, j: (i, j)),
        out_shape=jax.ShapeDtypeStruct((m, nj * tn), jnp.bfloat16),
        scratch_shapes=[pltpu.VMEM((tm, d), jnp.bfloat16)],
        compiler_params=pltpu.CompilerParams(
            dimension_semantics=("parallel", "arbitrary"), vmem_limit_bytes=VMEM_LIMIT_BYTES),
        name=name,
    )(*args)


def _lane_group_max(s):
    groups = [s[:, c * LANES:(c + 1) * LANES] for c in range(s.shape[1] // LANES)]
    return functools.reduce(jnp.maximum, groups)


def _diff_attn_kernel(q_ref, k_ref, v_ref, kc_ref, vc_ref, sg_ref, lam_ref, subg_ref, o_ref,
                      s_ref, pm_ref, m_ref, acc_ref, v1_ref, vc1_ref, *, tkb, lambda_init):
    seq = k_ref.shape[1]
    nkv = seq // tkb
    dv = v_ref.shape[2]
    q = q_ref[0]
    lane = lax.broadcasted_iota(jnp.int32, q.shape, 1)
    zero = jnp.zeros_like(q)
    qm = (jnp.where(lane < DA_QK_DIM, q, zero), jnp.where(lane >= DA_QK_DIM, q, zero))
    dims = (((1,), (1,)), ((), ()))

    @pl.when(pl.program_id(2) == 0)
    def _():
        v1_ref[:, :dv] = v_ref[0]
        v1_ref[:, dv:] = jnp.ones((seq, dv), v1_ref.dtype)
        vc1_ref[:, :dv] = vc_ref[0]
        vc1_ref[:, dv:] = jnp.ones((vc1_ref.shape[0], dv), vc1_ref.dtype)

    m_ref[...] = jnp.full(m_ref.shape, -jnp.inf, jnp.float32)
    acc_ref[...] = jnp.zeros(acc_ref.shape, jnp.float32)

    def scores(slot, jb):
        off = pl.multiple_of(jb * tkb, tkb)
        kblk = k_ref[0, pl.ds(off, tkb), :]
        for i in range(2):
            s = lax.dot_general(qm[i], kblk, dims, preferred_element_type=jnp.float32)
            s_ref[slot, i] = s
            pm_ref[slot, i] = _lane_group_max(s)

    def online_update(i, s, pm, v1blk):
        m_prev = m_ref[i]
        m_new = jnp.maximum(m_prev, jnp.max(pm, axis=1, keepdims=True))
        alpha = jnp.exp2(m_prev - m_new)
        p = jnp.exp2(s - m_new)
        acc_ref[i] = alpha * acc_ref[i] + jnp.dot(
            p.astype(jnp.bfloat16), v1blk, preferred_element_type=jnp.float32)
        m_ref[i] = m_new

    def consume(slot, jb):
        off = pl.multiple_of(jb * tkb, tkb)
        v1blk = v1_ref[pl.ds(off, tkb), :]
        for i in range(2):
            online_update(i, s_ref[slot, i], pm_ref[slot, i], v1blk)

    scores(0, 0)

    def body(jj, carry):
        jb = 2 * jj
        scores(1, jb + 1)
        consume(0, jb)
        scores(0, jb + 2)
        consume(1, jb + 1)
        return carry

    lax.fori_loop(0, nkv // 2 - 1, body, 0)
    scores(1, nkv - 1)
    consume(0, nkv - 2)
    kctx = kc_ref[0]
    s_ctx = [lax.dot_general(qm[i], kctx, dims, preferred_element_type=jnp.float32)
             for i in range(2)]
    consume(1, nkv - 1)
    for i in range(2):
        online_update(i, s_ctx[i], _lane_group_max(s_ctx[i]), vc1_ref[...])

    lam_p = lam_ref[...]
    lam = (jnp.exp(jnp.sum(lam_p[0:1] * lam_p[1:2], axis=1, keepdims=True))
           - jnp.exp(jnp.sum(lam_p[2:3] * lam_p[3:4], axis=1, keepdims=True)) + lambda_init)
    attn = (acc_ref[0, :, :dv] * (1.0 / acc_ref[0, :, dv:])
            - lam * (acc_ref[1, :, :dv] * (1.0 / acc_ref[1, :, dv:])))
    ms = jnp.mean(attn * attn, axis=-1, keepdims=True)
    y = attn * lax.rsqrt(ms + SUBLN_EPS) * subg_ref[...]
    y = y * (1.0 - lambda_init)
    o_ref[0] = (y * sg_ref[0].astype(jnp.float32)).astype(o_ref.dtype)


def _diff_attention(proj, ctxp, lam_p, subln_g, *, lambda_init, tq, tkb):
    b, seq, _ = proj.shape
    n_ctx = ctxp.shape[1]
    hb = DA_HEADS
    assert seq % (2 * tkb) == 0 and seq // tkb >= 2
    kernel = functools.partial(_diff_attn_kernel, tkb=tkb, lambda_init=lambda_init)
    return pl.pallas_call(
        kernel,
        grid=(b, DA_HEADS, seq // tq),
        in_specs=[
            pl.BlockSpec((1, tq, LANES), lambda bi, h, qi: (bi, qi, COL_QA * hb + h)),
            pl.BlockSpec((1, seq, LANES), lambda bi, h, qi: (bi, 0, COL_KA * hb + h)),
            pl.BlockSpec((1, seq, LANES), lambda bi, h, qi: (bi, 0, COL_VA * hb + h)),
            pl.BlockSpec((1, n_ctx, LANES), lambda bi, h, qi: (bi, 0, 0 * hb + h)),
            pl.BlockSpec((1, n_ctx, LANES), lambda bi, h, qi: (bi, 0, 1 * hb + h)),
            pl.BlockSpec((1, tq, LANES), lambda bi, h, qi: (bi, qi, COL_GA * hb + h)),
            pl.BlockSpec(lam_p.shape, lambda bi, h, qi: (0, 0)),
            pl.BlockSpec((1, LANES), lambda bi, h, qi: (0, 0)),
        ],
        out_specs=pl.BlockSpec((1, tq, LANES), lambda bi, h, qi: (bi, qi, h)),
        out_shape=jax.ShapeDtypeStruct((b, seq, DA_HEADS * DA_V_DIM), jnp.bfloat16),
        scratch_shapes=[
            pltpu.VMEM((2, 2, tq, tkb), jnp.float32),
            pltpu.VMEM((2, 2, tq, LANES), jnp.float32),
            pltpu.VMEM((2, tq, 1), jnp.float32),
            pltpu.VMEM((2, tq, 2 * DA_V_DIM), jnp.float32),
            pltpu.VMEM((seq, 2 * DA_V_DIM), jnp.bfloat16),
            pltpu.VMEM((n_ctx, 2 * DA_V_DIM), jnp.bfloat16),
        ],
        compiler_params=pltpu.CompilerParams(
            dimension_semantics=("parallel", "parallel", "arbitrary"),
            vmem_limit_bytes=VMEM_LIMIT_BYTES),
        name="diff_attention",
    )(proj, proj, proj, ctxp, ctxp, proj, lam_p, subln_g.reshape(1, LANES))


NA_BLOCK_ROWS = NA_WIN_ROWS // 2
NA_BAND_ROWS = 3 * NA_BLOCK_ROWS


def _na_bias_table(rpb_l, rows):
    r_blk, r_band = NA_BLOCK_ROWS, NA_BAND_ROWS
    n_blocks = rows // r_blk
    row_off = np.zeros((3, r_blk, r_band), np.int32)
    row_ok = np.zeros((3, r_blk, r_band), bool)
    for variant, rb in enumerate((0, 1, n_blocks - 1)):
        band0 = min(max(rb - 1, 0), n_blocks - 3) * r_blk
        for qr in range(r_blk):
            r = rb * r_blk + qr
            r0 = min(max(r - NA_WIN_ROWS // 2, 0), rows - NA_WIN_ROWS)
            for kr in range(r_band):
                ka = band0 + kr
                row_ok[variant, qr, kr] = r0 <= ka < r0 + NA_WIN_ROWS
                row_off[variant, qr, kr] = np.clip(ka - r + NA_WIN_ROWS - 1, 0, 2 * NA_WIN_ROWS - 2)
    qc = np.arange(GRID_W)
    c0 = np.clip(qc - NA_WIN_COLS // 2, 0, GRID_W - NA_WIN_COLS)
    kc = np.arange(GRID_W)
    col_ok = (kc[None, :] >= c0[:, None]) & (kc[None, :] < c0[:, None] + NA_WIN_COLS)
    heads, n_ro, _ = rpb_l.shape
    pad = GRID_W - NA_WIN_COLS
    rp = jnp.pad(rpb_l.astype(jnp.float32) * LOG2E, ((0, 0), (0, 0), (pad, pad)))
    tiles = jnp.stack([rp[:, :, GRID_W - 1 - q:2 * GRID_W - 1 - q] for q in range(GRID_W)],
                      axis=2)
    tiles = jnp.where(jnp.asarray(col_ok)[None, None], tiles, MASK_VALUE)
    masked = jnp.full((heads, GRID_W, GRID_W), MASK_VALUE, jnp.float32)
    variants = []
    for variant in range(3):
        block_rows = []
        for qr in range(r_blk):
            block_rows.append(jnp.concatenate(
                [tiles[:, row_off[variant, qr, kr]] if row_ok[variant, qr, kr] else masked
                 for kr in range(r_band)], axis=-1))
        variants.append(jnp.concatenate(block_rows, axis=-2))
    return jnp.stack(variants, axis=1)


def _na_kernel(q_ref, k_ref, v_ref, kc_ref, vc_ref, sg_ref, bias_ref, o_ref):
    seq = q_ref.shape[1]
    tq = NA_BLOCK_ROWS * GRID_W
    band = NA_BAND_ROWS * GRID_W
    n_blocks = seq // tq
    kctx = kc_ref[0]
    vctx = vc_ref[0]
    dims = (((1,), (1,)), ((), ()))

    def body(rb, carry):
        q_off = pl.multiple_of(rb * tq, tq)
        k_off = pl.multiple_of(jnp.clip(rb - 1, 0, n_blocks - 3) * tq, tq)
        variant = jnp.where(rb == 0, 0, jnp.where(rb == n_blocks - 1, 2, 1))
        q = q_ref[0, pl.ds(q_off, tq), :]
        s_w = lax.dot_general(q, k_ref[0, pl.ds(k_off, band), :], dims,
                              preferred_element_type=jnp.float32) + bias_ref[0, variant]
        s_c = lax.dot_general(q, kctx, dims, preferred_element_type=jnp.float32)
        m = jnp.maximum(jnp.max(s_w, axis=1, keepdims=True), jnp.max(s_c, axis=1, keepdims=True))
        p_w = jnp.exp2(s_w - m)
        p_c = jnp.exp2(s_c - m)
        l = jnp.sum(p_w, axis=1, keepdims=True) + jnp.sum(p_c, axis=1, keepdims=True)
        o = (jnp.dot(p_w.astype(jnp.bfloat16), v_ref[0, pl.ds(k_off, band), :],
                     preferred_element_type=jnp.float32)
             + jnp.dot(p_c.astype(jnp.bfloat16), vctx, preferred_element_type=jnp.float32))
        o = o * (1.0 / l)
        o_ref[0, pl.ds(q_off, tq), :] = (
            o * sg_ref[0, pl.ds(q_off, tq), :].astype(jnp.float32)).astype(o_ref.dtype)
        return carry

    lax.fori_loop(0, n_blocks, body, 0)


def _neighbourhood_attention(proj, ctxp, bias):
    b, seq, _ = proj.shape
    n_ctx = ctxp.shape[1]
    hb = NA_HEADS
    tq = NA_BLOCK_ROWS * GRID_W
    band = NA_BAND_ROWS * GRID_W
    return pl.pallas_call(
        _na_kernel,
        grid=(b, NA_HEADS),
        in_specs=[
            pl.BlockSpec((1, seq, LANES), lambda bi, h: (bi, 0, COL_QB * hb + h)),
            pl.BlockSpec((1, seq, LANES), lambda bi, h: (bi, 0, COL_KB * hb + h)),
            pl.BlockSpec((1, seq, LANES), lambda bi, h: (bi, 0, COL_VB * hb + h)),
            pl.BlockSpec((1, n_ctx, LANES), lambda bi, h: (bi, 0, 2 * hb + h)),
            pl.BlockSpec((1, n_ctx, LANES), lambda bi, h: (bi, 0, 3 * hb + h)),
            pl.BlockSpec((1, seq, LANES), lambda bi, h: (bi, 0, COL_GB * hb + h)),
            pl.BlockSpec((1, 3, tq, band), lambda bi, h: (h, 0, 0, 0)),
        ],
        out_specs=pl.BlockSpec((1, seq, LANES), lambda bi, h: (bi, 0, h)),
        out_shape=jax.ShapeDtypeStruct((b, seq, NA_HEADS * NA_HEAD_DIM), jnp.bfloat16),
        compiler_params=pltpu.CompilerParams(
            dimension_semantics=("parallel", "parallel"), vmem_limit_bytes=VMEM_LIMIT_BYTES),
        name="neighbourhood_attention",
    )(proj, proj, proj, ctxp, ctxp, proj, bias)


def _out_kernel(oa_ref, ob_ref, wa_ref, wb_ref, x_ref, gate_ref, fg_ref, o_ref, *, final_norm):
    mixed = (jnp.dot(oa_ref[...], wa_ref[...], preferred_element_type=jnp.float32)
             + jnp.dot(ob_ref[...], wb_ref[...], preferred_element_type=jnp.float32))
    h = x_ref[...] + gate_ref[0] * mixed
    if final_norm:
        ms = jnp.mean(h * h, axis=-1, keepdims=True)
        h = h * lax.rsqrt(ms + NORM_EPS) * fg_ref[...]
    o_ref[...] = h


def _out_projection(oa2, ob2, w_out_bf16, x2, gate, final_g, *, tm, rows_per_mod, final_norm):
    m, d = x2.shape
    wa_rows = oa2.shape[1]
    wb_rows = ob2.shape[1]
    assert wa_rows == wb_rows
    blocks_per_mod = rows_per_mod // tm
    kernel = functools.partial(_out_kernel, final_norm=final_norm)
    return pl.pallas_call(
        kernel,
        grid=(m // tm,),
        in_specs=[
            pl.BlockSpec((tm, wa_rows), lambda i: (i, 0)),
            pl.BlockSpec((tm, wb_rows), lambda i: (i, 0)),
            pl.BlockSpec((wa_rows, d), lambda i: (0, 0)),
            pl.BlockSpec((wb_rows, d), lambda i: (1, 0)),
            pl.BlockSpec((tm, d), lambda i: (i, 0)),
            pl.BlockSpec((1, 1, d), lambda i: (i // blocks_per_mod, 0, 0)),
            pl.BlockSpec((1, d), lambda i: (0, 0)),
        ],
        out_specs=pl.BlockSpec((tm, d), lambda i: (i, 0)),
        out_shape=jax.ShapeDtypeStruct((m, d), jnp.float32),
        compiler_params=pltpu.CompilerParams(
            dimension_semantics=("parallel",), vmem_limit_bytes=VMEM_LIMIT_BYTES),
        name="out_projection",
    )(oa2, ob2, w_out_bf16, w_out_bf16, x2, gate, final_g.reshape(1, d))


def kernel(x, c, ctx, c_ctx, norm_g, w_mod, b_mod, w_in, w_out, lam_q1, lam_k1, lam_q2, lam_k2,
           subln_g, rpb, final_g):
    b, seq, d = x.shape
    n_ctx = ctx.shape[1]
    depth = w_in.shape[0]
    rows = seq // GRID_W
    group = DA_HEADS * DA_V_DIM
    assert w_in.shape[2] == 8 * group and NA_HEADS * NA_HEAD_DIM == group
    assert DA_V_DIM == LANES and NA_HEAD_DIM == LANES

    rope_tabs = _rope_tables(seq)
    q_scales = (DA_QK_DIM ** -0.5 * LOG2E, NA_HEAD_DIM ** -0.5 * LOG2E)
    x_kinds = ("rope_q", "rope_k", "plain", "silu", "scale_qb", "plain", "plain", "silu")
    ctx_cols = (COL_KA, COL_VA, COL_KB, COL_VB)

    mod_rows = 8
    cc = jnp.zeros((mod_rows, d), jnp.float32).at[:b].set(c).at[b].set(c_ctx)
    ctx2 = ctx.reshape(b * n_ctx, d)

    h2 = x.reshape(b * seq, d)
    for layer in range(depth):
        lambda_init = 0.8 - 0.6 * math.exp(-0.3 * layer)
        mod = _modulation(cc, w_mod[layer], b_mod[layer])
        shift = mod[:b, :d].reshape(b, 1, d)
        scale = mod[:b, d:2 * d].reshape(b, 1, d)
        gate = mod[:b, 2 * d:].reshape(b, 1, d)
        shift_c = mod[b:b + 1, :d].reshape(1, 1, d)
        scale_c = mod[b:b + 1, d:2 * d].reshape(1, 1, d)

        wi = w_in[layer].astype(jnp.bfloat16)
        proj = _in_projection(
            h2, norm_g[layer], scale, shift, wi, tm=1024, tn=group, rows_per_mod=seq,
            col_blocks=tuple(range(8)), kinds=x_kinds, q_scales=q_scales, rope_tabs=rope_tabs,
            name="in_projection").reshape(b, seq, 8 * group)
        ctxp = _in_projection(
            ctx2, norm_g[layer], scale_c, shift_c, wi, tm=b * n_ctx, tn=group,
            rows_per_mod=b * n_ctx, col_blocks=ctx_cols, kinds=("plain",) * 4, q_scales=q_scales,
            rope_tabs=None, name="ctx_projection").reshape(b, n_ctx, 4 * group)

        lam_p = jnp.stack([lam_q1[layer], lam_k1[layer], lam_q2[layer], lam_k2[layer]]
                          ).astype(jnp.float32)
        oa = _diff_attention(proj, ctxp, lam_p, subln_g[layer], lambda_init=lambda_init,
                             tq=512, tkb=1024)
        ob = _neighbourhood_attention(proj, ctxp, _na_bias_table(rpb[layer], rows))

        h2 = _out_projection(
            oa.reshape(b * seq, group), ob.reshape(b * seq, group),
            w_out[layer].astype(jnp.bfloat16), h2, gate, final_g, tm=512, rows_per_mod=seq,
            final_norm=(layer == depth - 1))
    return h2.reshape(b, seq, d)
```

```python
import functools
import math

import numpy as np
import jax
import jax.numpy as jnp
from jax import lax
from jax.experimental import pallas as pl
from jax.experimental.pallas import tpu as pltpu

GRID_W = 64
DA_HEADS = 8
DA_QK_DIM = 64
DA_V_DIM = 2 * DA_QK_DIM
NA_HEADS = 8
NA_HEAD_DIM = 128
NA_WIN_ROWS = 8
NA_WIN_COLS = 16
ROPE_BASE = 10000.0
NORM_EPS = 1e-6
SUBLN_EPS = 1e-5

LANES = 128
VMEM_LIMIT_BYTES = 56 * 1024 * 1024
LOG2E = math.log2(math.e)
MASK_VALUE = -1e30

COL_QA, COL_KA, COL_VA, COL_GA, COL_QB, COL_KB, COL_VB, COL_GB = range(8)


def _silu(x):
    return x * (1.0 / (1.0 + jnp.exp(-x)))


def _mod_kernel(c_ref, w_ref, b_ref, o_ref):
    a = _silu(c_ref[...]).astype(jnp.bfloat16)
    w = w_ref[...].astype(jnp.bfloat16)
    o_ref[...] = jnp.dot(a, w, preferred_element_type=jnp.float32) + b_ref[...]


def _modulation(cc, w_mod, b_mod):
    rows, d = cc.shape
    n = w_mod.shape[1]
    tn = 768
    return pl.pallas_call(
        _mod_kernel,
        grid=(n // tn,),
        in_specs=[
            pl.BlockSpec((rows, d), lambda j: (0, 0)),
            pl.BlockSpec((d, tn), lambda j: (0, j)),
            pl.BlockSpec((1, tn), lambda j: (0, j)),
        ],
        out_specs=pl.BlockSpec((rows, tn), lambda j: (0, j)),
        out_shape=jax.ShapeDtypeStruct((rows, n), jnp.float32),
        compiler_params=pltpu.CompilerParams(
            dimension_semantics=("parallel",), vmem_limit_bytes=VMEM_LIMIT_BYTES),
        name="modulation",
    )(cc, w_mod, b_mod.reshape(1, n))


def _rope_tables(seq):
    t = jnp.arange(seq, dtype=jnp.int32)
    row_pos = (t // GRID_W).astype(jnp.float32)
    col_pos = (t % GRID_W).astype(jnp.float32)
    half = DA_QK_DIM // 4
    inv_freq = ROPE_BASE ** (-jnp.arange(half, dtype=jnp.float32) / half)
    lane = np.arange(LANES)
    d = lane % DA_QK_DIM
    use_col = (d // (DA_QK_DIM // 2)) == 1
    f = d % half
    first_half = (d % (2 * half)) < half
    pos = jnp.where(jnp.asarray(use_col)[None, :], col_pos[:, None], row_pos[:, None])
    ang = pos * inv_freq[jnp.asarray(f)][None, :]
    cos = jnp.cos(ang)
    sin = jnp.sin(ang)
    sin_signed = jnp.where(jnp.asarray(first_half)[None, :], -sin, sin)
    return cos, sin_signed


def _proj_kernel(*refs, kinds, q_scales, with_rope):
    if with_rope:
        x_ref, g_ref, scale_ref, shift_ref, w_ref, cos_ref, sin_ref, o_ref, hx_ref = refs
    else:
        x_ref, g_ref, scale_ref, shift_ref, w_ref, o_ref, hx_ref = refs
    j = pl.program_id(1)

    @pl.when(j == 0)
    def _():
        x = x_ref[...]
        ms = jnp.mean(x * x, axis=-1, keepdims=True)
        y = x * lax.rsqrt(ms + NORM_EPS) * g_ref[...]
        hx = y * (1.0 + scale_ref[0]) + shift_ref[0]
        hx_ref[...] = hx.astype(jnp.bfloat16)

    acc = jnp.dot(hx_ref[...], w_ref[...], preferred_element_type=jnp.float32)
    tn = acc.shape[1]
    if all(kind == "plain" for kind in kinds):
        o_ref[...] = acc.astype(o_ref.dtype)
        return

    def any_of(names):
        hits = [j == jj for jj, kind in enumerate(kinds) if kind in names]
        return functools.reduce(jnp.logical_or, hits)

    is_rope = any_of(("rope_q", "rope_k"))
    is_silu = any_of(("silu",))
    col_scale = jnp.float32(1.0)
    for jj, kind in enumerate(kinds):
        if kind == "rope_q":
            col_scale = jnp.where(j == jj, jnp.float32(q_scales[0]), col_scale)
        elif kind == "scale_qb":
            col_scale = jnp.where(j == jj, jnp.float32(q_scales[1]), col_scale)
    a = jnp.where(is_rope, cos_ref[...], 1.0) * col_scale
    b = jnp.where(is_rope, sin_ref[...], 0.0) * col_scale
    lane = lax.broadcasted_iota(jnp.int32, a.shape, 1)
    half = DA_QK_DIM // 4
    first_half = (lane % (2 * half)) < half
    for hh in range(tn // LANES):
        xs = acc[:, hh * LANES:(hh + 1) * LANES]
        fwd = pltpu.roll(xs, LANES - half, 1)
        bwd = pltpu.roll(xs, half, 1)
        y = xs * a + jnp.where(first_half, fwd, bwd) * b
        y = jnp.where(is_silu, _silu(y), y)
        o_ref[:, hh * LANES:(hh + 1) * LANES] = y.astype(o_ref.dtype)


def _in_projection(x2, norm_g, scale, shift, w_bf16, *, tm, tn, rows_per_mod, col_blocks,
                   kinds, q_scales, rope_tabs, name):
    m, d = x2.shape
    nj = len(col_blocks)
    with_rope = rope_tabs is not None
    blocks_per_mod = rows_per_mod // tm
    start, skip = col_blocks[0], 0
    if nj > 2:
        skip = col_blocks[2] - col_blocks[1] - 1
    assert all(col_blocks[jj] == start + jj + skip * (jj // 2) for jj in range(nj))

    def w_map(i, j):
        return (0, start + j + skip * (j // 2))

    in_specs = [
        pl.BlockSpec((tm, d), lambda i, j: (i, 0)),
        pl.BlockSpec((1, d), lambda i, j: (0, 0)),
        pl.BlockSpec((1, 1, d), lambda i, j: (i // blocks_per_mod, 0, 0)),
        pl.BlockSpec((1, 1, d), lambda i, j: (i // blocks_per_mod, 0, 0)),
        pl.BlockSpec((d, tn), w_map),
    ]
    args = [x2, norm_g.reshape(1, d), scale, shift, w_bf16]
    if with_rope:
        seq_blocks = rope_tabs[0].shape[0] // tm
        in_specs += [pl.BlockSpec((tm, LANES), lambda i, j: (i % seq_blocks, 0))] * 2
        args += list(rope_tabs)
    kernel = functools.partial(_proj_kernel, kinds=tuple(kinds), q_scales=q_scales,
                               with_rope=with_rope)
    return pl.pallas_call(
        kernel,
        grid=(m // tm, nj),
        in_specs=in_specs,
        out_specs=pl.BlockSpec((tm, tn), lambda i, j: (i, j)),
        out_shape=jax.ShapeDtypeStruct((m, nj * tn), jnp.bfloat16),
        scratch_shapes=[pltpu.VMEM((tm, d), jnp.bfloat16)],
        compiler_params=pltpu.CompilerParams(
            dimension_semantics=("parallel", "arbitrary"), vmem_limit_bytes=VMEM_LIMIT_BYTES),
        name=name,
    )(*args)


def _diff_attn_kernel(q_ref, k_ref, v_ref, kc_ref, vc_ref, sg_ref, lam_ref, subg_ref, o_ref,
                      s_ref, cm_ref, m_ref, l_ref, acc_ref, vt_ref, vct_ref, *, tkb, lambda_init):
    seq = k_ref.shape[1]
    nkv = seq // tkb
    q = q_ref[0]
    lane = lax.broadcasted_iota(jnp.int32, q.shape, 1)
    zero = jnp.zeros_like(q)
    qm = (jnp.where(lane < DA_QK_DIM, q, zero), jnp.where(lane >= DA_QK_DIM, q, zero))
    dims = (((1,), (1,)), ((), ()))

    @pl.when(pl.program_id(2) == 0)
    def _():
        for c in range(nkv):
            vt_ref[:, c * tkb:(c + 1) * tkb] = v_ref[0, c * tkb:(c + 1) * tkb, :].T
        vct_ref[...] = vc_ref[0].T

    m_ref[...] = jnp.full(m_ref.shape, -jnp.inf, jnp.float32)
    l_ref[...] = jnp.zeros(l_ref.shape, jnp.float32)
    acc_ref[...] = jnp.zeros(acc_ref.shape, jnp.float32)

    def scores_t(kblk, i):
        return lax.dot_general(kblk, qm[i], dims, preferred_element_type=jnp.float32)

    def scores(slot, jb):
        off = pl.multiple_of(jb * tkb, tkb)
        kblk = k_ref[0, pl.ds(off, tkb), :]
        for i in range(2):
            st = scores_t(kblk, i)
            s_ref[slot, i] = st
            cm_ref[slot, i] = jnp.max(st, axis=0, keepdims=True)

    def online_update(i, st, cm, vt_blk):
        m_prev = m_ref[i]
        m_new = jnp.maximum(m_prev, cm)
        alpha = jnp.exp2(m_prev - m_new)
        p = jnp.exp2(st - m_new)
        l_ref[i] = alpha * l_ref[i] + jnp.sum(p, axis=0, keepdims=True)
        acc_ref[i] = alpha * acc_ref[i] + jnp.dot(
            vt_blk, p.astype(jnp.bfloat16), preferred_element_type=jnp.float32)
        m_ref[i] = m_new

    def consume(slot, jb):
        off = pl.multiple_of(jb * tkb, tkb)
        vt_blk = vt_ref[:, pl.ds(off, tkb)]
        for i in range(2):
            online_update(i, s_ref[slot, i], cm_ref[slot, i], vt_blk)

    scores(0, 0)

    def body(jj, carry):
        jb = 2 * jj
        scores(1, jb + 1)
        consume(0, jb)
        scores(0, jb + 2)
        consume(1, jb + 1)
        return carry

    lax.fori_loop(0, nkv // 2 - 1, body, 0)
    scores(1, nkv - 1)
    consume(0, nkv - 2)
    kctx = kc_ref[0]
    s_ctx = [scores_t(kctx, i) for i in range(2)]
    consume(1, nkv - 1)
    for i in range(2):
        online_update(i, s_ctx[i], jnp.max(s_ctx[i], axis=0, keepdims=True), vct_ref[...])

    lam_p = lam_ref[...]
    lam = (jnp.exp(jnp.sum(lam_p[0:1] * lam_p[1:2], axis=1, keepdims=True))
           - jnp.exp(jnp.sum(lam_p[2:3] * lam_p[3:4], axis=1, keepdims=True)) + lambda_init)
    attn_t = (acc_ref[0] * (1.0 / l_ref[0]) - lam * (acc_ref[1] * (1.0 / l_ref[1])))
    attn = attn_t.T
    ms = jnp.mean(attn * attn, axis=-1, keepdims=True)
    y = attn * lax.rsqrt(ms + SUBLN_EPS) * subg_ref[...]
    y = y * (1.0 - lambda_init)
    o_ref[0] = (y * sg_ref[0].astype(jnp.float32)).astype(o_ref.dtype)


def _diff_attention(proj, ctxp, lam_p, subln_g, *, lambda_init, tq, tkb):
    b, seq, _ = proj.shape
    n_ctx = ctxp.shape[1]
    hb = DA_HEADS
    assert seq % (2 * tkb) == 0 and seq // tkb >= 2
    kernel = functools.partial(_diff_attn_kernel, tkb=tkb, lambda_init=lambda_init)
    return pl.pallas_call(
        kernel,
        grid=(b, DA_HEADS, seq // tq),
        in_specs=[
            pl.BlockSpec((1, tq, LANES), lambda bi, h, qi: (bi, qi, COL_QA * hb + h)),
            pl.BlockSpec((1, seq, LANES), lambda bi, h, qi: (bi, 0, COL_KA * hb + h)),
            pl.BlockSpec((1, seq, LANES), lambda bi, h, qi: (bi, 0, COL_VA * hb + h)),
            pl.BlockSpec((1, n_ctx, LANES), lambda bi, h, qi: (bi, 0, 0 * hb + h)),
            pl.BlockSpec((1, n_ctx, LANES), lambda bi, h, qi: (bi, 0, 1 * hb + h)),
            pl.BlockSpec((1, tq, LANES), lambda bi, h, qi: (bi, qi, COL_GA * hb + h)),
            pl.BlockSpec(lam_p.shape, lambda bi, h, qi: (0, 0)),
            pl.BlockSpec((1, LANES), lambda bi, h, qi: (0, 0)),
        ],
        out_specs=pl.BlockSpec((1, tq, LANES), lambda bi, h, qi: (bi, qi, h)),
        out_shape=jax.ShapeDtypeStruct((b, seq, DA_HEADS * DA_V_DIM), jnp.bfloat16),
        scratch_shapes=[
            pltpu.VMEM((2, 2, tkb, tq), jnp.float32),
            pltpu.VMEM((2, 2, 1, tq), jnp.float32),
            pltpu.VMEM((2, 1, tq), jnp.float32),
            pltpu.VMEM((2, 1, tq), jnp.float32),
            pltpu.VMEM((2, DA_V_DIM, tq), jnp.float32),
            pltpu.VMEM((DA_V_DIM, seq), jnp.bfloat16),
            pltpu.VMEM((DA_V_DIM, n_ctx), jnp.bfloat16),
        ],
        compiler_params=pltpu.CompilerParams(
            dimension_semantics=("parallel", "parallel", "arbitrary"),
            vmem_limit_bytes=VMEM_LIMIT_BYTES),
        name="diff_attention",
    )(proj, proj, proj, ctxp, ctxp, proj, lam_p, subln_g.reshape(1, LANES))


NA_BLOCK_ROWS = NA_WIN_ROWS // 2
NA_BAND_ROWS = 3 * NA_BLOCK_ROWS


def _na_bias_table(rpb_l, rows):
    r_blk, r_band = NA_BLOCK_ROWS, NA_BAND_ROWS
    n_blocks = rows // r_blk
    row_off = np.zeros((3, r_blk, r_band), np.int32)
    row_ok = np.zeros((3, r_blk, r_band), bool)
    for variant, rb in enumerate((0, 1, n_blocks - 1)):
        band0 = min(max(rb - 1, 0), n_blocks - 3) * r_blk
        for qr in range(r_blk):
            r = rb * r_blk + qr
            r0 = min(max(r - NA_WIN_ROWS // 2, 0), rows - NA_WIN_ROWS)
            for kr in range(r_band):
                ka = band0 + kr
                row_ok[variant, qr, kr] = r0 <= ka < r0 + NA_WIN_ROWS
                row_off[variant, qr, kr] = np.clip(ka - r + NA_WIN_ROWS - 1, 0, 2 * NA_WIN_ROWS - 2)
    qc = np.arange(GRID_W)
    c0 = np.clip(qc - NA_WIN_COLS // 2, 0, GRID_W - NA_WIN_COLS)
    kc = np.arange(GRID_W)
    col_ok = (kc[None, :] >= c0[:, None]) & (kc[None, :] < c0[:, None] + NA_WIN_COLS)
    heads, n_ro, _ = rpb_l.shape
    pad = GRID_W - NA_WIN_COLS
    rp = jnp.pad(rpb_l.astype(jnp.float32) * LOG2E, ((0, 0), (0, 0), (pad, pad)))
    tiles = jnp.stack([rp[:, :, GRID_W - 1 - q:2 * GRID_W - 1 - q] for q in range(GRID_W)],
                      axis=2)
    tiles = jnp.where(jnp.asarray(col_ok)[None, None], tiles, MASK_VALUE)
    masked = jnp.full((heads, GRID_W, GRID_W), MASK_VALUE, jnp.float32)
    variants = []
    for variant in range(3):
        block_rows = []
        for qr in range(r_blk):
            block_rows.append(jnp.concatenate(
                [tiles[:, row_off[variant, qr, kr]] if row_ok[variant, qr, kr] else masked
                 for kr in range(r_band)], axis=-1))
        variants.append(jnp.concatenate(block_rows, axis=-2))
    return jnp.stack(variants, axis=1)


def _na_kernel(q_ref, k_ref, v_ref, kc_ref, vc_ref, sg_ref, bias_ref, o_ref):
    seq = q_ref.shape[1]
    tq = NA_BLOCK_ROWS * GRID_W
    band = NA_BAND_ROWS * GRID_W
    n_blocks = seq // tq
    kctx = kc_ref[0]
    vctx = vc_ref[0]
    dims = (((1,), (1,)), ((), ()))

    def body(rb, carry):
        q_off = pl.multiple_of(rb * tq, tq)
        k_off = pl.multiple_of(jnp.clip(rb - 1, 0, n_blocks - 3) * tq, tq)
        variant = jnp.where(rb == 0, 0, jnp.where(rb == n_blocks - 1, 2, 1))
        q = q_ref[0, pl.ds(q_off, tq), :]
        s_w = lax.dot_general(q, k_ref[0, pl.ds(k_off, band), :], dims,
                              preferred_element_type=jnp.float32) + bias_ref[0, variant]
        s_c = lax.dot_general(q, kctx, dims, preferred_element_type=jnp.float32)
        m = jnp.maximum(jnp.max(s_w, axis=1, keepdims=True), jnp.max(s_c, axis=1, keepdims=True))
        p_w = jnp.exp2(s_w - m)
        p_c = jnp.exp2(s_c - m)
        l = jnp.sum(p_w, axis=1, keepdims=True) + jnp.sum(p_c, axis=1, keepdims=True)
        o = (jnp.dot(p_w.astype(jnp.bfloat16), v_ref[0, pl.ds(k_off, band), :],
                     preferred_element_type=jnp.float32)
             + jnp.dot(p_c.astype(jnp.bfloat16), vctx, preferred_element_type=jnp.float32))
        o = o * (1.0 / l)
        o_ref[0, pl.ds(q_off, tq), :] = (
            o * sg_ref[0, pl.ds(q_off, tq), :].astype(jnp.float32)).astype(o_ref.dtype)
        return carry

    lax.fori_loop(0, n_blocks, body, 0, unroll=2)


def _neighbourhood_attention(proj, ctxp, bias):
    b, seq, _ = proj.shape
    n_ctx = ctxp.shape[1]
    hb = NA_HEADS
    tq = NA_BLOCK_ROWS * GRID_W
    band = NA_BAND_ROWS * GRID_W
    return pl.pallas_call(
        _na_kernel,
        grid=(b, NA_HEADS),
        in_specs=[
            pl.BlockSpec((1, seq, LANES), lambda bi, h: (bi, 0, COL_QB * hb + h)),
            pl.BlockSpec((1, seq, LANES), lambda bi, h: (bi, 0, COL_KB * hb + h)),
            pl.BlockSpec((1, seq, LANES), lambda bi, h: (bi, 0, COL_VB * hb + h)),
            pl.BlockSpec((1, n_ctx, LANES), lambda bi, h: (bi, 0, 2 * hb + h)),
            pl.BlockSpec((1, n_ctx, LANES), lambda bi, h: (bi, 0, 3 * hb + h)),
            pl.BlockSpec((1, seq, LANES), lambda bi, h: (bi, 0, COL_GB * hb + h)),
            pl.BlockSpec((1, 3, tq, band), lambda bi, h: (h, 0, 0, 0)),
        ],
        out_specs=pl.BlockSpec((1, seq, LANES), lambda bi, h: (bi, 0, h)),
        out_shape=jax.ShapeDtypeStruct((b, seq, NA_HEADS * NA_HEAD_DIM), jnp.bfloat16),
        compiler_params=pltpu.CompilerParams(
            dimension_semantics=("parallel", "parallel"), vmem_limit_bytes=VMEM_LIMIT_BYTES),
        name="neighbourhood_attention",
    )(proj, proj, proj, ctxp, ctxp, proj, bias)


def _out_kernel(oa_ref, ob_ref, wa_ref, wb_ref, x_ref, gate_ref, fg_ref, o_ref, *, final_norm):
    mixed = (jnp.dot(oa_ref[...], wa_ref[...], preferred_element_type=jnp.float32)
             + jnp.dot(ob_ref[...], wb_ref[...], preferred_element_type=jnp.float32))
    h = x_ref[...] + gate_ref[0] * mixed
    if final_norm:
        ms = jnp.mean(h * h, axis=-1, keepdims=True)
        h = h * lax.rsqrt(ms + NORM_EPS) * fg_ref[...]
    o_ref[...] = h


def _out_projection(oa2, ob2, w_out_bf16, x2, gate, final_g, *, tm, rows_per_mod, final_norm):
    m, d = x2.shape
    wa_rows = oa2.shape[1]
    wb_rows = ob2.shape[1]
    assert wa_rows == wb_rows
    blocks_per_mod = rows_per_mod // tm
    kernel = functools.partial(_out_kernel, final_norm=final_norm)
    return pl.pallas_call(
        kernel,
        grid=(m // tm,),
        in_specs=[
            pl.BlockSpec((tm, wa_rows), lambda i: (i, 0)),
            pl.BlockSpec((tm, wb_rows), lambda i: (i, 0)),
            pl.BlockSpec((wa_rows, d), lambda i: (0, 0)),
            pl.BlockSpec((wb_rows, d), lambda i: (1, 0)),
            pl.BlockSpec((tm, d), lambda i: (i, 0)),
            pl.BlockSpec((1, 1, d), lambda i: (i // blocks_per_mod, 0, 0)),
            pl.BlockSpec((1, d), lambda i: (0, 0)),
        ],
        out_specs=pl.BlockSpec((tm, d), lambda i: (i, 0)),
        out_shape=jax.ShapeDtypeStruct((m, d), jnp.float32),
        compiler_params=pltpu.CompilerParams(
            dimension_semantics=("parallel",), vmem_limit_bytes=VMEM_LIMIT_BYTES),
        name="out_projection",
    )(oa2, ob2, w_out_bf16, w_out_bf16, x2, gate, final_g.reshape(1, d))


def kernel(x, c, ctx, c_ctx, norm_g, w_mod, b_mod, w_in, w_out, lam_q1, lam_k1, lam_q2, lam_k2,
           subln_g, rpb, final_g):
    b, seq, d = x.shape
    n_ctx = ctx.shape[1]
    depth = w_in.shape[0]
    rows = seq // GRID_W
    group = DA_HEADS * DA_V_DIM
    assert w_in.shape[2] == 8 * group and NA_HEADS * NA_HEAD_DIM == group
    assert DA_V_DIM == LANES and NA_HEAD_DIM == LANES

    rope_tabs = _rope_tables(seq)
    q_scales = (DA_QK_DIM ** -0.5 * LOG2E, NA_HEAD_DIM ** -0.5 * LOG2E)
    x_kinds = ("rope_q", "rope_k", "plain", "silu", "scale_qb", "plain", "plain", "silu")
    ctx_cols = (COL_KA, COL_VA, COL_KB, COL_VB)

    mod_rows = 8
    cc = jnp.zeros((mod_rows, d), jnp.float32).at[:b].set(c).at[b].set(c_ctx)
    ctx2 = ctx.reshape(b * n_ctx, d)

    h2 = x.reshape(b * seq, d)
    for layer in range(depth):
        lambda_init = 0.8 - 0.6 * math.exp(-0.3 * layer)
        mod = _modulation(cc, w_mod[layer], b_mod[layer])
        shift = mod[:b, :d].reshape(b, 1, d)
        scale = mod[:b, d:2 * d].reshape(b, 1, d)
        gate = mod[:b, 2 * d:].reshape(b, 1, d)
        shift_c = mod[b:b + 1, :d].reshape(1, 1, d)
        scale_c = mod[b:b + 1, d:2 * d].reshape(1, 1, d)

        wi = w_in[layer].astype(jnp.bfloat16)
        proj = _in_projection(
            h2, norm_g[layer], scale, shift, wi, tm=1024, tn=group, rows_per_mod=seq,
            col_blocks=tuple(range(8)), kinds=x_kinds, q_scales=q_scales, rope_tabs=rope_tabs,
            name="in_projection").reshape(b, seq, 8 * group)
        ctxp = _in_projection(
            ctx2, norm_g[layer], scale_c, shift_c, wi, tm=b * n_ctx, tn=group,
            rows_per_mod=b * n_ctx, col_blocks=ctx_cols, kinds=("plain",) * 4, q_scales=q_scales,
            rope_tabs=None, name="ctx_projection").reshape(b, n_ctx, 4 * group)

        lam_p = jnp.stack([lam_q1[layer], lam_k1[layer], lam_q2[layer], lam_k2[layer]]
                          ).astype(jnp.float32)
        oa = _diff_attention(proj, ctxp, lam_p, subln_g[layer], lambda_init=lambda_init,
                             tq=512, tkb=1024)
        ob = _neighbourhood_attention(proj, ctxp, _na_bias_table(rpb[layer], rows))

        h2 = _out_projection(
            oa.reshape(b * seq, group), ob.reshape(b * seq, group),
            w_out[layer].astype(jnp.bfloat16), h2, gate, final_g, tm=512, rows_per_mod=seq,
            final_norm=(layer == depth - 1))
    return h2.reshape(b, seq, d)
```

```python
import functools
import math

import numpy as np
import jax
import jax.numpy as jnp
from jax import lax
from jax.experimental import pallas as pl
from jax.experimental.pallas import tpu as pltpu

GRID_W = 64
DA_HEADS = 8
DA_QK_DIM = 64
DA_V_DIM = 2 * DA_QK_DIM
NA_HEADS = 8
NA_HEAD_DIM = 128
NA_WIN_ROWS = 8
NA_WIN_COLS = 16
ROPE_BASE = 10000.0
NORM_EPS = 1e-6
SUBLN_EPS = 1e-5

LANES = 128
VMEM_LIMIT_BYTES = 56 * 1024 * 1024
LOG2E = math.log2(math.e)
MASK_VALUE = -1e30

COL_QA, COL_KA, COL_VA, COL_GA, COL_QB, COL_KB, COL_VB, COL_GB = range(8)


def _silu(x):
    return x * (1.0 / (1.0 + jnp.exp(-x)))


def _mod_kernel(c_ref, w_ref, b_ref, o_ref):
    a = _silu(c_ref[...]).astype(jnp.bfloat16)
    w = w_ref[...].astype(jnp.bfloat16)
    o_ref[...] = jnp.dot(a, w, preferred_element_type=jnp.float32) + b_ref[...]


def _modulation(cc, w_mod, b_mod):
    rows, d = cc.shape
    n = w_mod.shape[1]
    tn = 768
    return pl.pallas_call(
        _mod_kernel,
        grid=(n // tn,),
        in_specs=[
            pl.BlockSpec((rows, d), lambda j: (0, 0)),
            pl.BlockSpec((d, tn), lambda j: (0, j)),
            pl.BlockSpec((1, tn), lambda j: (0, j)),
        ],
        out_specs=pl.BlockSpec((rows, tn), lambda j: (0, j)),
        out_shape=jax.ShapeDtypeStruct((rows, n), jnp.float32),
        compiler_params=pltpu.CompilerParams(
            dimension_semantics=("parallel",), vmem_limit_bytes=VMEM_LIMIT_BYTES),
        name="modulation",
    )(cc, w_mod, b_mod.reshape(1, n))


def _rope_tables(seq):
    t = jnp.arange(seq, dtype=jnp.int32)
    row_pos = (t // GRID_W).astype(jnp.float32)
    col_pos = (t % GRID_W).astype(jnp.float32)
    half = DA_QK_DIM // 4
    inv_freq = ROPE_BASE ** (-jnp.arange(half, dtype=jnp.float32) / half)
    lane = np.arange(LANES)
    d = lane % DA_QK_DIM
    use_col = (d // (DA_QK_DIM // 2)) == 1
    f = d % half
    first_half = (d % (2 * half)) < half
    pos = jnp.where(jnp.asarray(use_col)[None, :], col_pos[:, None], row_pos[:, None])
    ang = pos * inv_freq[jnp.asarray(f)][None, :]
    cos = jnp.cos(ang)
    sin = jnp.sin(ang)
    sin_signed = jnp.where(jnp.asarray(first_half)[None, :], -sin, sin)
    return cos, sin_signed


def _proj_kernel(*refs, kinds, q_scales, with_rope):
    if with_rope:
        x_ref, g_ref, scale_ref, shift_ref, w_ref, cos_ref, sin_ref, o_ref, hx_ref = refs
    else:
        x_ref, g_ref, scale_ref, shift_ref, w_ref, o_ref, hx_ref = refs
    j = pl.program_id(1)

    @pl.when(j == 0)
    def _():
        x = x_ref[...]
        ms = jnp.mean(x * x, axis=-1, keepdims=True)
        y = x * lax.rsqrt(ms + NORM_EPS) * g_ref[...]
        hx = y * (1.0 + scale_ref[0]) + shift_ref[0]
        hx_ref[...] = hx.astype(jnp.bfloat16)

    acc = jnp.dot(hx_ref[...], w_ref[...], preferred_element_type=jnp.float32)
    tn = acc.shape[1]
    if all(kind == "plain" for kind in kinds):
        o_ref[...] = acc.astype(o_ref.dtype)
        return

    def any_of(names):
        hits = [j == jj for jj, kind in enumerate(kinds) if kind in names]
        return functools.reduce(jnp.logical_or, hits)

    is_rope = any_of(("rope_q", "rope_k"))
    is_silu = any_of(("silu",))
    col_scale = jnp.float32(1.0)
    for jj, kind in enumerate(kinds):
        if kind == "rope_q":
            col_scale = jnp.where(j == jj, jnp.float32(q_scales[0]), col_scale)
        elif kind == "scale_qb":
            col_scale = jnp.where(j == jj, jnp.float32(q_scales[1]), col_scale)
    a = jnp.where(is_rope, cos_ref[...], 1.0) * col_scale
    b = jnp.where(is_rope, sin_ref[...], 0.0) * col_scale
    lane = lax.broadcasted_iota(jnp.int32, a.shape, 1)
    half = DA_QK_DIM // 4
    first_half = (lane % (2 * half)) < half
    for hh in range(tn // LANES):
        xs = acc[:, hh * LANES:(hh + 1) * LANES]
        fwd = pltpu.roll(xs, LANES - half, 1)
        bwd = pltpu.roll(xs, half, 1)
        y = xs * a + jnp.where(first_half, fwd, bwd) * b
        y = jnp.where(is_silu, _silu(y), y)
        o_ref[:, hh * LANES:(hh + 1) * LANES] = y.astype(o_ref.dtype)


def _in_projection(x2, norm_g, scale, shift, w_bf16, *, tm, tn, rows_per_mod, col_blocks,
                   kinds, q_scales, rope_tabs, name):
    m, d = x2.shape
    nj = len(col_blocks)
    with_rope = rope_tabs is not None
    blocks_per_mod = rows_per_mod // tm
    start, skip = col_blocks[0], 0
    if nj > 2:
        skip = col_blocks[2] - col_blocks[1] - 1
    assert all(col_blocks[jj] == start + jj + skip * (jj // 2) for jj in range(nj))

    def w_map(i, j):
        return (0, start + j + skip * (j // 2))

    in_specs = [
        pl.BlockSpec((tm, d), lambda i, j: (i, 0)),
        pl.BlockSpec((1, d), lambda i, j: (0, 0)),
        pl.BlockSpec((1, 1, d), lambda i, j: (i // blocks_per_mod, 0, 0)),
        pl.BlockSpec((1, 1, d), lambda i, j: (i // blocks_per_mod, 0, 0)),
        pl.BlockSpec((d, tn), w_map),
    ]
    args = [x2, norm_g.reshape(1, d), scale, shift, w_bf16]
    if with_rope:
        seq_blocks = rope_tabs[0].shape[0] // tm
        in_specs += [pl.BlockSpec((tm, LANES), lambda i, j: (i % seq_blocks, 0))] * 2
        args += list(rope_tabs)
    kernel = functools.partial(_proj_kernel, kinds=tuple(kinds), q_scales=q_scales,
                               with_rope=with_rope)
    return pl.pallas_call(
        kernel,
        grid=(m // tm, nj),
        in_specs=in_specs,
        out_specs=pl.BlockSpec((tm, tn), lambda i, j: (i, j)),
        out_shape=jax.ShapeDtypeStruct((m, nj * tn), jnp.bfloat16),
        scratch_shapes=[pltpu.VMEM((tm, d), jnp.bfloat16)],
        compiler_params=pltpu.CompilerParams(
            dimension_semantics=("parallel", "arbitrary"), vmem_limit_bytes=VMEM_LIMIT_BYTES),
        name=name,
    )(*args)


def _diff_attn_kernel(q_ref, k_ref, v_ref, kc_ref, vc_ref, sg_ref, lam_ref, subg_ref, o_ref,
                      s_ref, cm_ref, m_ref, l_ref, acc_ref, vt_ref, vct_ref, *, tkb, lambda_init):
    seq = k_ref.shape[1]
    tq = s_ref.shape[3]
    nkv = seq // tkb
    nq = seq // tq
    dims = (((1,), (1,)), ((), ()))

    for c in range(nkv):
        vt_ref[:, c * tkb:(c + 1) * tkb] = v_ref[0, c * tkb:(c + 1) * tkb, :].T
    vct_ref[...] = vc_ref[0].T

    lam_p = lam_ref[...]
    lam = (jnp.exp(jnp.sum(lam_p[0:1] * lam_p[1:2], axis=1, keepdims=True))
           - jnp.exp(jnp.sum(lam_p[2:3] * lam_p[3:4], axis=1, keepdims=True)) + lambda_init)

    def masked_q(qi):
        q = q_ref[0, pl.ds(pl.multiple_of(qi * tq, tq), tq), :]
        lane = lax.broadcasted_iota(jnp.int32, q.shape, 1)
        zero = jnp.zeros_like(q)
        return (jnp.where(lane < DA_QK_DIM, q, zero), jnp.where(lane >= DA_QK_DIM, q, zero))

    def scores_t(kblk, qmi):
        return lax.dot_general(kblk, qmi, dims, preferred_element_type=jnp.float32)

    def scores(slot, jb, qm):
        off = pl.multiple_of(jb * tkb, tkb)
        kblk = k_ref[0, pl.ds(off, tkb), :]
        for i in range(2):
            st = scores_t(kblk, qm[i])
            s_ref[slot, i] = st
            cm_ref[slot, i] = jnp.max(st, axis=0, keepdims=True)

    def online_update(i, st, cm, vt_blk):
        m_prev = m_ref[i]
        m_new = jnp.maximum(m_prev, cm)
        alpha = jnp.exp2(m_prev - m_new)
        p = jnp.exp2(st - m_new)
        l_ref[i] = alpha * l_ref[i] + jnp.sum(p, axis=0, keepdims=True)
        acc_ref[i] = alpha * acc_ref[i] + jnp.dot(
            vt_blk, p.astype(jnp.bfloat16), preferred_element_type=jnp.float32)
        m_ref[i] = m_new

    def consume(slot, jb):
        off = pl.multiple_of(jb * tkb, tkb)
        vt_blk = vt_ref[:, pl.ds(off, tkb)]
        for i in range(2):
            online_update(i, s_ref[slot, i], cm_ref[slot, i], vt_blk)

    scores(0, 0, masked_q(0))

    def q_block(qi, carry):
        qm = masked_q(qi)
        m_ref[...] = jnp.full(m_ref.shape, -jnp.inf, jnp.float32)
        l_ref[...] = jnp.zeros(l_ref.shape, jnp.float32)
        acc_ref[...] = jnp.zeros(acc_ref.shape, jnp.float32)

        for jb in range(nkv - 1):
            scores((jb + 1) % 2, jb + 1, qm)
            consume(jb % 2, jb)
        kctx = kc_ref[0]
        s_ctx = [scores_t(kctx, qm[i]) for i in range(2)]
        scores(0, 0, masked_q(jnp.minimum(qi + 1, nq - 1)))
        consume((nkv - 1) % 2, nkv - 1)
        for i in range(2):
            online_update(i, s_ctx[i], jnp.max(s_ctx[i], axis=0, keepdims=True), vct_ref[...])

        attn_t = (acc_ref[0] * (1.0 / l_ref[0]) - lam * (acc_ref[1] * (1.0 / l_ref[1])))
        attn = attn_t.T
        ms = jnp.mean(attn * attn, axis=-1, keepdims=True)
        y = attn * lax.rsqrt(ms + SUBLN_EPS) * subg_ref[...]
        y = y * (1.0 - lambda_init)
        rows = pl.ds(pl.multiple_of(qi * tq, tq), tq)
        o_ref[0, rows, :] = (y * sg_ref[0, rows, :].astype(jnp.float32)).astype(o_ref.dtype)
        return carry

    lax.fori_loop(0, nq, q_block, 0)


def _diff_attention(proj, ctxp, lam_p, subln_g, *, lambda_init, tq, tkb):
    b, seq, _ = proj.shape
    n_ctx = ctxp.shape[1]
    hb = DA_HEADS
    assert seq % (2 * tkb) == 0 and seq // tkb >= 2
    kernel = functools.partial(_diff_attn_kernel, tkb=tkb, lambda_init=lambda_init)
    return pl.pallas_call(
        kernel,
        grid=(b, DA_HEADS),
        in_specs=[
            pl.BlockSpec((1, seq, LANES), lambda bi, h: (bi, 0, COL_QA * hb + h)),
            pl.BlockSpec((1, seq, LANES), lambda bi, h: (bi, 0, COL_KA * hb + h)),
            pl.BlockSpec((1, seq, LANES), lambda bi, h: (bi, 0, COL_VA * hb + h)),
            pl.BlockSpec((1, n_ctx, LANES), lambda bi, h: (bi, 0, 0 * hb + h)),
            pl.BlockSpec((1, n_ctx, LANES), lambda bi, h: (bi, 0, 1 * hb + h)),
            pl.BlockSpec((1, seq, LANES), lambda bi, h: (bi, 0, COL_GA * hb + h)),
            pl.BlockSpec(lam_p.shape, lambda bi, h: (0, 0)),
            pl.BlockSpec((1, LANES), lambda bi, h: (0, 0)),
        ],
        out_specs=pl.BlockSpec((1, seq, LANES), lambda bi, h: (bi, 0, h)),
        out_shape=jax.ShapeDtypeStruct((b, seq, DA_HEADS * DA_V_DIM), jnp.bfloat16),
        scratch_shapes=[
            pltpu.VMEM((2, 2, tkb, tq), jnp.float32),
            pltpu.VMEM((2, 2, 1, tq), jnp.float32),
            pltpu.VMEM((2, 1, tq), jnp.float32),
            pltpu.VMEM((2, 1, tq), jnp.float32),
            pltpu.VMEM((2, DA_V_DIM, tq), jnp.float32),
            pltpu.VMEM((DA_V_DIM, seq), jnp.bfloat16),
            pltpu.VMEM((DA_V_DIM, n_ctx), jnp.bfloat16),
        ],
        compiler_params=pltpu.CompilerParams(
            dimension_semantics=("parallel", "parallel"), vmem_limit_bytes=VMEM_LIMIT_BYTES),
        name="diff_attention",
    )(proj, proj, proj, ctxp, ctxp, proj, lam_p, subln_g.reshape(1, LANES))


NA_BLOCK_ROWS = NA_WIN_ROWS // 2
NA_BAND_ROWS = 3 * NA_BLOCK_ROWS


def _na_bias_table(rpb_l, rows):
    r_blk, r_band = NA_BLOCK_ROWS, NA_BAND_ROWS
    n_blocks = rows // r_blk
    row_off = np.zeros((3, r_blk, r_band), np.int32)
    row_ok = np.zeros((3, r_blk, r_band), bool)
    for variant, rb in enumerate((0, 1, n_blocks - 1)):
        band0 = min(max(rb - 1, 0), n_blocks - 3) * r_blk
        for qr in range(r_blk):
            r = rb * r_blk + qr
            r0 = min(max(r - NA_WIN_ROWS // 2, 0), rows - NA_WIN_ROWS)
            for kr in range(r_band):
                ka = band0 + kr
                row_ok[variant, qr, kr] = r0 <= ka < r0 + NA_WIN_ROWS
                row_off[variant, qr, kr] = np.clip(ka - r + NA_WIN_ROWS - 1, 0, 2 * NA_WIN_ROWS - 2)
    qc = np.arange(GRID_W)
    c0 = np.clip(qc - NA_WIN_COLS // 2, 0, GRID_W - NA_WIN_COLS)
    kc = np.arange(GRID_W)
    col_ok = (kc[None, :] >= c0[:, None]) & (kc[None, :] < c0[:, None] + NA_WIN_COLS)
    heads, n_ro, _ = rpb_l.shape
    pad = GRID_W - NA_WIN_COLS
    rp = jnp.pad(rpb_l.astype(jnp.float32) * LOG2E, ((0, 0), (0, 0), (pad, pad)))
    tiles = jnp.stack([rp[:, :, GRID_W - 1 - q:2 * GRID_W - 1 - q] for q in range(GRID_W)],
                      axis=2)
    tiles = jnp.where(jnp.asarray(col_ok)[None, None], tiles, MASK_VALUE)
    masked = jnp.full((heads, GRID_W, GRID_W), MASK_VALUE, jnp.float32)
    variants = []
    for variant in range(3):
        block_rows = []
        for qr in range(r_blk):
            block_rows.append(jnp.concatenate(
                [tiles[:, row_off[variant, qr, kr]] if row_ok[variant, qr, kr] else masked
                 for kr in range(r_band)], axis=-1))
        variants.append(jnp.concatenate(block_rows, axis=-2))
    return jnp.stack(variants, axis=1)


def _na_kernel(q_ref, k_ref, v_ref, kc_ref, vc_ref, sg_ref, bias_ref, o_ref):
    seq = q_ref.shape[1]
    tq = NA_BLOCK_ROWS * GRID_W
    band = NA_BAND_ROWS * GRID_W
    n_blocks = seq // tq
    kctx = kc_ref[0]
    vctx = vc_ref[0]
    dims = (((1,), (1,)), ((), ()))

    def body(rb, carry):
        q_off = pl.multiple_of(rb * tq, tq)
        k_off = pl.multiple_of(jnp.clip(rb - 1, 0, n_blocks - 3) * tq, tq)
        variant = jnp.where(rb == 0, 0, jnp.where(rb == n_blocks - 1, 2, 1))
        q = q_ref[0, pl.ds(q_off, tq), :]
        s_w = lax.dot_general(q, k_ref[0, pl.ds(k_off, band), :], dims,
                              preferred_element_type=jnp.float32) + bias_ref[0, variant]
        s_c = lax.dot_general(q, kctx, dims, preferred_element_type=jnp.float32)
        m = jnp.maximum(jnp.max(s_w, axis=1, keepdims=True), jnp.max(s_c, axis=1, keepdims=True))
        p_w = jnp.exp2(s_w - m)
        p_c = jnp.exp2(s_c - m)
        l = jnp.sum(p_w, axis=1, keepdims=True) + jnp.sum(p_c, axis=1, keepdims=True)
        o = (jnp.dot(p_w.astype(jnp.bfloat16), v_ref[0, pl.ds(k_off, band), :],
                     preferred_element_type=jnp.float32)
             + jnp.dot(p_c.astype(jnp.bfloat16), vctx, preferred_element_type=jnp.float32))
        o = o * (1.0 / l)
        o_ref[0, pl.ds(q_off, tq), :] = (
            o * sg_ref[0, pl.ds(q_off, tq), :].astype(jnp.float32)).astype(o_ref.dtype)
        return carry

    lax.fori_loop(0, n_blocks, body, 0, unroll=2)


def _neighbourhood_attention(proj, ctxp, bias):
    b, seq, _ = proj.shape
    n_ctx = ctxp.shape[1]
    hb = NA_HEADS
    tq = NA_BLOCK_ROWS * GRID_W
    band = NA_BAND_ROWS * GRID_W
    return pl.pallas_call(
        _na_kernel,
        grid=(b, NA_HEADS),
        in_specs=[
            pl.BlockSpec((1, seq, LANES), lambda bi, h: (bi, 0, COL_QB * hb + h)),
            pl.BlockSpec((1, seq, LANES), lambda bi, h: (bi, 0, COL_KB * hb + h)),
            pl.BlockSpec((1, seq, LANES), lambda bi, h: (bi, 0, COL_VB * hb + h)),
            pl.BlockSpec((1, n_ctx, LANES), lambda bi, h: (bi, 0, 2 * hb + h)),
            pl.BlockSpec((1, n_ctx, LANES), lambda bi, h: (bi, 0, 3 * hb + h)),
            pl.BlockSpec((1, seq, LANES), lambda bi, h: (bi, 0, COL_GB * hb + h)),
            pl.BlockSpec((1, 3, tq, band), lambda bi, h: (h, 0, 0, 0)),
        ],
        out_specs=pl.BlockSpec((1, seq, LANES), lambda bi, h: (bi, 0, h)),
        out_shape=jax.ShapeDtypeStruct((b, seq, NA_HEADS * NA_HEAD_DIM), jnp.bfloat16),
        compiler_params=pltpu.CompilerParams(
            dimension_semantics=("parallel", "parallel"), vmem_limit_bytes=VMEM_LIMIT_BYTES),
        name="neighbourhood_attention",
    )(proj, proj, proj, ctxp, ctxp, proj, bias)


def _out_kernel(oa_ref, ob_ref, wa_ref, wb_ref, x_ref, gate_ref, fg_ref, o_ref, *, final_norm):
    mixed = (jnp.dot(oa_ref[...], wa_ref[...], preferred_element_type=jnp.float32)
             + jnp.dot(ob_ref[...], wb_ref[...], preferred_element_type=jnp.float32))
    h = x_ref[...] + gate_ref[0] * mixed
    if final_norm:
        ms = jnp.mean(h * h, axis=-1, keepdims=True)
        h = h * lax.rsqrt(ms + NORM_EPS) * fg_ref[...]
    o_ref[...] = h


def _out_projection(oa2, ob2, w_out_bf16, x2, gate, final_g, *, tm, rows_per_mod, final_norm):
    m, d = x2.shape
    wa_rows = oa2.shape[1]
    wb_rows = ob2.shape[1]
    assert wa_rows == wb_rows
    blocks_per_mod = rows_per_mod // tm
    kernel = functools.partial(_out_kernel, final_norm=final_norm)
    return pl.pallas_call(
        kernel,
        grid=(m // tm,),
        in_specs=[
            pl.BlockSpec((tm, wa_rows), lambda i: (i, 0)),
            pl.BlockSpec((tm, wb_rows), lambda i: (i, 0)),
            pl.BlockSpec((wa_rows, d), lambda i: (0, 0)),
            pl.BlockSpec((wb_rows, d), lambda i: (1, 0)),
            pl.BlockSpec((tm, d), lambda i: (i, 0)),
            pl.BlockSpec((1, 1, d), lambda i: (i // blocks_per_mod, 0, 0)),
            pl.BlockSpec((1, d), lambda i: (0, 0)),
        ],
        out_specs=pl.BlockSpec((tm, d), lambda i: (i, 0)),
        out_shape=jax.ShapeDtypeStruct((m, d), jnp.float32),
        compiler_params=pltpu.CompilerParams(
            dimension_semantics=("parallel",), vmem_limit_bytes=VMEM_LIMIT_BYTES),
        name="out_projection",
    )(oa2, ob2, w_out_bf16, w_out_bf16, x2, gate, final_g.reshape(1, d))


def kernel(x, c, ctx, c_ctx, norm_g, w_mod, b_mod, w_in, w_out, lam_q1, lam_k1, lam_q2, lam_k2,
           subln_g, rpb, final_g):
    b, seq, d = x.shape
    n_ctx = ctx.shape[1]
    depth = w_in.shape[0]
    rows = seq // GRID_W
    group = DA_HEADS * DA_V_DIM
    assert w_in.shape[2] == 8 * group and NA_HEADS * NA_HEAD_DIM == group
    assert DA_V_DIM == LANES and NA_HEAD_DIM == LANES

    rope_tabs = _rope_tables(seq)
    q_scales = (DA_QK_DIM ** -0.5 * LOG2E, NA_HEAD_DIM ** -0.5 * LOG2E)
    x_kinds = ("rope_q", "rope_k", "plain", "silu", "scale_qb", "plain", "plain", "silu")
    ctx_cols = (COL_KA, COL_VA, COL_KB, COL_VB)

    mod_rows = 8
    cc = jnp.zeros((mod_rows, d), jnp.float32).at[:b].set(c).at[b].set(c_ctx)
    ctx2 = ctx.reshape(b * n_ctx, d)

    h2 = x.reshape(b * seq, d)
    for layer in range(depth):
        lambda_init = 0.8 - 0.6 * math.exp(-0.3 * layer)
        mod = _modulation(cc, w_mod[layer], b_mod[layer])
        shift = mod[:b, :d].reshape(b, 1, d)
        scale = mod[:b, d:2 * d].reshape(b, 1, d)
        gate = mod[:b, 2 * d:].reshape(b, 1, d)
        shift_c = mod[b:b + 1, :d].reshape(1, 1, d)
        scale_c = mod[b:b + 1, d:2 * d].reshape(1, 1, d)

        wi = w_in[layer].astype(jnp.bfloat16)
        proj = _in_projection(
            h2, norm_g[layer], scale, shift, wi, tm=1024, tn=group, rows_per_mod=seq,
            col_blocks=tuple(range(8)), kinds=x_kinds, q_scales=q_scales, rope_tabs=rope_tabs,
            name="in_projection").reshape(b, seq, 8 * group)
        ctxp = _in_projection(
            ctx2, norm_g[layer], scale_c, shift_c, wi, tm=b * n_ctx, tn=group,
            rows_per_mod=b * n_ctx, col_blocks=ctx_cols, kinds=("plain",) * 4, q_scales=q_scales,
            rope_tabs=None, name="ctx_projection").reshape(b, n_ctx, 4 * group)

        lam_p = jnp.stack([lam_q1[layer], lam_k1[layer], lam_q2[layer], lam_k2[layer]]
                          ).astype(jnp.float32)
        oa = _diff_attention(proj, ctxp, lam_p, subln_g[layer], lambda_init=lambda_init,
                             tq=512, tkb=512)
        ob = _neighbourhood_attention(proj, ctxp, _na_bias_table(rpb[layer], rows))

        h2 = _out_projection(
            oa.reshape(b * seq, group), ob.reshape(b * seq, group),
            w_out[layer].astype(jnp.bfloat16), h2, gate, final_g, tm=512, rows_per_mod=seq,
            final_norm=(layer == depth - 1))
    return h2.reshape(b, seq, d)
```

```python
import functools
import math

import numpy as np
import jax
import jax.numpy as jnp
from jax import lax
from jax.experimental import pallas as pl
from jax.experimental.pallas import tpu as pltpu

GRID_W = 64
DA_HEADS = 8
DA_QK_DIM = 64
DA_V_DIM = 2 * DA_QK_DIM
NA_HEADS = 8
NA_HEAD_DIM = 128
NA_WIN_ROWS = 8
NA_WIN_COLS = 16
ROPE_BASE = 10000.0
NORM_EPS = 1e-6
SUBLN_EPS = 1e-5

LANES = 128
VMEM_LIMIT_BYTES = 56 * 1024 * 1024
LOG2E = math.log2(math.e)
MASK_VALUE = -1e30

COL_QA, COL_KA, COL_VA, COL_GA, COL_QB, COL_KB, COL_VB, COL_GB = range(8)


def _silu(x):
    return x * (1.0 / (1.0 + jnp.exp(-x)))


def _mod_kernel(c_ref, w_ref, b_ref, o_ref):
    a = _silu(c_ref[...]).astype(jnp.bfloat16)
    w = w_ref[...].astype(jnp.bfloat16)
    o_ref[...] = jnp.dot(a, w, preferred_element_type=jnp.float32) + b_ref[...]


def _modulation(cc, w_mod, b_mod):
    rows, d = cc.shape
    n = w_mod.shape[1]
    tn = 768
    return pl.pallas_call(
        _mod_kernel,
        grid=(n // tn,),
        in_specs=[
            pl.BlockSpec((rows, d), lambda j: (0, 0)),
            pl.BlockSpec((d, tn), lambda j: (0, j)),
            pl.BlockSpec((1, tn), lambda j: (0, j)),
        ],
        out_specs=pl.BlockSpec((rows, tn), lambda j: (0, j)),
        out_shape=jax.ShapeDtypeStruct((rows, n), jnp.float32),
        compiler_params=pltpu.CompilerParams(
            dimension_semantics=("parallel",), vmem_limit_bytes=VMEM_LIMIT_BYTES),
        name="modulation",
    )(cc, w_mod, b_mod.reshape(1, n))


def _rope_tables(seq):
    t = jnp.arange(seq, dtype=jnp.int32)
    row_pos = (t // GRID_W).astype(jnp.float32)
    col_pos = (t % GRID_W).astype(jnp.float32)
    half = DA_QK_DIM // 4
    inv_freq = ROPE_BASE ** (-jnp.arange(half, dtype=jnp.float32) / half)
    lane = np.arange(LANES)
    d = lane % DA_QK_DIM
    use_col = (d // (DA_QK_DIM // 2)) == 1
    f = d % half
    first_half = (d % (2 * half)) < half
    pos = jnp.where(jnp.asarray(use_col)[None, :], col_pos[:, None], row_pos[:, None])
    ang = pos * inv_freq[jnp.asarray(f)][None, :]
    cos = jnp.cos(ang)
    sin = jnp.sin(ang)
    sin_signed = jnp.where(jnp.asarray(first_half)[None, :], -sin, sin)
    return cos, sin_signed


def _proj_kernel(*refs, kinds, q_scales, with_rope):
    if with_rope:
        x_ref, g_ref, scale_ref, shift_ref, w_ref, cos_ref, sin_ref, o_ref, hx_ref = refs
    else:
        x_ref, g_ref, scale_ref, shift_ref, w_ref, o_ref, hx_ref = refs
    j = pl.program_id(1)

    @pl.when(j == 0)
    def _():
        x = x_ref[...]
        ms = jnp.mean(x * x, axis=-1, keepdims=True)
        y = x * lax.rsqrt(ms + NORM_EPS) * g_ref[...]
        hx = y * (1.0 + scale_ref[0]) + shift_ref[0]
        hx_ref[...] = hx.astype(jnp.bfloat16)

    acc = jnp.dot(hx_ref[...], w_ref[...], preferred_element_type=jnp.float32)
    tn = acc.shape[1]
    if all(kind == "plain" for kind in kinds):
        o_ref[...] = acc.astype(o_ref.dtype)
        return

    def any_of(names):
        hits = [j == jj for jj, kind in enumerate(kinds) if kind in names]
        return functools.reduce(jnp.logical_or, hits)

    is_rope = any_of(("rope_q", "rope_k"))
    is_silu = any_of(("silu",))
    col_scale = jnp.float32(1.0)
    for jj, kind in enumerate(kinds):
        if kind == "rope_q":
            col_scale = jnp.where(j == jj, jnp.float32(q_scales[0]), col_scale)
        elif kind == "scale_qb":
            col_scale = jnp.where(j == jj, jnp.float32(q_scales[1]), col_scale)
    a = jnp.where(is_rope, cos_ref[...], 1.0) * col_scale
    b = jnp.where(is_rope, sin_ref[...], 0.0) * col_scale
    lane = lax.broadcasted_iota(jnp.int32, a.shape, 1)
    half = DA_QK_DIM // 4
    first_half = (lane % (2 * half)) < half
    for hh in range(tn // LANES):
        xs = acc[:, hh * LANES:(hh + 1) * LANES]
        fwd = pltpu.roll(xs, LANES - half, 1)
        bwd = pltpu.roll(xs, half, 1)
        y = xs * a + jnp.where(first_half, fwd, bwd) * b
        y = jnp.where(is_silu, _silu(y), y)
        o_ref[:, hh * LANES:(hh + 1) * LANES] = y.astype(o_ref.dtype)


def _in_projection(x2, norm_g, scale, shift, w_bf16, *, tm, tn, rows_per_mod, col_blocks,
                   kinds, q_scales, rope_tabs, name):
    m, d = x2.shape
    nj = len(col_blocks)
    with_rope = rope_tabs is not None
    blocks_per_mod = rows_per_mod // tm
    start, skip = col_blocks[0], 0
    if nj > 2:
        skip = col_blocks[2] - col_blocks[1] - 1
    assert all(col_blocks[jj] == start + jj + skip * (jj // 2) for jj in range(nj))

    def w_map(i, j):
        return (0, start + j + skip * (j // 2))

    in_specs = [
        pl.BlockSpec((tm, d), lambda i, j: (i, 0)),
        pl.BlockSpec((1, d), lambda i, j: (0, 0)),
        pl.BlockSpec((1, 1, d), lambda i, j: (i // blocks_per_mod, 0, 0)),
        pl.BlockSpec((1, 1, d), lambda i, j: (i // blocks_per_mod, 0, 0)),
        pl.BlockSpec((d, tn), w_map),
    ]
    args = [x2, norm_g.reshape(1, d), scale, shift, w_bf16]
    if with_rope:
        seq_blocks = rope_tabs[0].shape[0] // tm
        in_specs += [pl.BlockSpec((tm, LANES), lambda i, j: (i % seq_blocks, 0))] * 2
        args += list(rope_tabs)
    kernel = functools.partial(_proj_kernel, kinds=tuple(kinds), q_scales=q_scales,
                               with_rope=with_rope)
    return pl.pallas_call(
        kernel,
        grid=(m // tm, nj),
        in_specs=in_specs,
        out_specs=pl.BlockSpec((tm, tn), lambda i, j: (i, j)),
        out_shape=jax.ShapeDtypeStruct((m, nj * tn), jnp.bfloat16),
        scratch_shapes=[pltpu.VMEM((tm, d), jnp.bfloat16)],
        compiler_params=pltpu.CompilerParams(
            dimension_semantics=("parallel", "arbitrary"), vmem_limit_bytes=VMEM_LIMIT_BYTES),
        name=name,
    )(*args)


def _diff_attn_kernel(q_ref, k_ref, v_ref, kc_ref, vc_ref, sg_ref, lam_ref, subg_ref, o_ref,
                      s_ref, cm_ref, m_ref, acc_ref, vt_ref, vct_ref, *, tkb, lambda_init):
    seq = k_ref.shape[1]
    tq = s_ref.shape[3]
    nkv = seq // tkb
    nq = seq // tq
    dims = (((1,), (1,)), ((), ()))

    dv = v_ref.shape[2]
    for c in range(nkv):
        vt_ref[:dv, c * tkb:(c + 1) * tkb] = v_ref[0, c * tkb:(c + 1) * tkb, :].T
    vt_ref[dv:, :] = jnp.ones((vt_ref.shape[0] - dv, seq), vt_ref.dtype)
    vct_ref[:dv, :] = vc_ref[0].T
    vct_ref[dv:, :] = jnp.ones((vct_ref.shape[0] - dv, vct_ref.shape[1]), vct_ref.dtype)

    lam_p = lam_ref[...]
    lam = (jnp.exp(jnp.sum(lam_p[0:1] * lam_p[1:2], axis=1, keepdims=True))
           - jnp.exp(jnp.sum(lam_p[2:3] * lam_p[3:4], axis=1, keepdims=True)) + lambda_init)

    def masked_q(qi):
        q = q_ref[0, pl.ds(pl.multiple_of(qi * tq, tq), tq), :]
        lane = lax.broadcasted_iota(jnp.int32, q.shape, 1)
        zero = jnp.zeros_like(q)
        return (jnp.where(lane < DA_QK_DIM, q, zero), jnp.where(lane >= DA_QK_DIM, q, zero))

    def scores_t(kblk, qmi):
        return lax.dot_general(kblk, qmi, dims, preferred_element_type=jnp.float32)

    def scores(slot, jb, qm):
        off = pl.multiple_of(jb * tkb, tkb)
        kblk = k_ref[0, pl.ds(off, tkb), :]
        for i in range(2):
            st = scores_t(kblk, qm[i])
            s_ref[slot, i] = st
            cm_ref[slot, i] = jnp.max(st, axis=0, keepdims=True)

    def online_update(i, st, cm, vt_blk):
        m_prev = m_ref[i]
        m_new = jnp.maximum(m_prev, cm)
        alpha = jnp.exp2(m_prev - m_new)
        p = jnp.exp2(st - m_new)
        acc_ref[i] = alpha * acc_ref[i] + jnp.dot(
            vt_blk, p.astype(jnp.bfloat16), preferred_element_type=jnp.float32)
        m_ref[i] = m_new

    def consume(slot, jb):
        off = pl.multiple_of(jb * tkb, tkb)
        vt_blk = vt_ref[:, pl.ds(off, tkb)]
        for i in range(2):
            online_update(i, s_ref[slot, i], cm_ref[slot, i], vt_blk)

    scores(0, 0, masked_q(0))

    def q_block(qi, carry):
        qm = masked_q(qi)
        m_ref[...] = jnp.full(m_ref.shape, -jnp.inf, jnp.float32)
        acc_ref[...] = jnp.zeros(acc_ref.shape, jnp.float32)

        for jb in range(nkv - 1):
            scores((jb + 1) % 2, jb + 1, qm)
            consume(jb % 2, jb)
        kctx = kc_ref[0]
        s_ctx = [scores_t(kctx, qm[i]) for i in range(2)]
        scores(0, 0, masked_q(jnp.minimum(qi + 1, nq - 1)))
        consume((nkv - 1) % 2, nkv - 1)
        for i in range(2):
            online_update(i, s_ctx[i], jnp.max(s_ctx[i], axis=0, keepdims=True), vct_ref[...])

        attn_t = (acc_ref[0, :dv] * (1.0 / acc_ref[0, dv:dv + 1])
                  - lam * (acc_ref[1, :dv] * (1.0 / acc_ref[1, dv:dv + 1])))
        attn = attn_t.T
        ms = jnp.mean(attn * attn, axis=-1, keepdims=True)
        y = attn * lax.rsqrt(ms + SUBLN_EPS) * subg_ref[...]
        y = y * (1.0 - lambda_init)
        rows = pl.ds(pl.multiple_of(qi * tq, tq), tq)
        o_ref[0, rows, :] = (y * sg_ref[0, rows, :].astype(jnp.float32)).astype(o_ref.dtype)
        return carry

    lax.fori_loop(0, nq, q_block, 0)


def _diff_attention(proj, ctxp, lam_p, subln_g, *, lambda_init, tq, tkb):
    b, seq, _ = proj.shape
    n_ctx = ctxp.shape[1]
    hb = DA_HEADS
    assert seq % (2 * tkb) == 0 and seq // tkb >= 2
    ones_rows = 16
    kernel = functools.partial(_diff_attn_kernel, tkb=tkb, lambda_init=lambda_init)
    return pl.pallas_call(
        kernel,
        grid=(b, DA_HEADS),
        in_specs=[
            pl.BlockSpec((1, seq, LANES), lambda bi, h: (bi, 0, COL_QA * hb + h)),
            pl.BlockSpec((1, seq, LANES), lambda bi, h: (bi, 0, COL_KA * hb + h)),
            pl.BlockSpec((1, seq, LANES), lambda bi, h: (bi, 0, COL_VA * hb + h)),
            pl.BlockSpec((1, n_ctx, LANES), lambda bi, h: (bi, 0, 0 * hb + h)),
            pl.BlockSpec((1, n_ctx, LANES), lambda bi, h: (bi, 0, 1 * hb + h)),
            pl.BlockSpec((1, seq, LANES), lambda bi, h: (bi, 0, COL_GA * hb + h)),
            pl.BlockSpec(lam_p.shape, lambda bi, h: (0, 0)),
            pl.BlockSpec((1, LANES), lambda bi, h: (0, 0)),
        ],
        out_specs=pl.BlockSpec((1, seq, LANES), lambda bi, h: (bi, 0, h)),
        out_shape=jax.ShapeDtypeStruct((b, seq, DA_HEADS * DA_V_DIM), jnp.bfloat16),
        scratch_shapes=[
            pltpu.VMEM((2, 2, tkb, tq), jnp.float32),
            pltpu.VMEM((2, 2, 1, tq), jnp.float32),
            pltpu.VMEM((2, 1, tq), jnp.float32),
            pltpu.VMEM((2, DA_V_DIM + ones_rows, tq), jnp.float32),
            pltpu.VMEM((DA_V_DIM + ones_rows, seq), jnp.bfloat16),
            pltpu.VMEM((DA_V_DIM + ones_rows, n_ctx), jnp.bfloat16),
        ],
        compiler_params=pltpu.CompilerParams(
            dimension_semantics=("parallel", "parallel"), vmem_limit_bytes=VMEM_LIMIT_BYTES),
        name="diff_attention",
    )(proj, proj, proj, ctxp, ctxp, proj, lam_p, subln_g.reshape(1, LANES))


NA_BLOCK_ROWS = NA_WIN_ROWS // 2
NA_BAND_ROWS = 3 * NA_BLOCK_ROWS


def _na_bias_table(rpb_l, rows):
    r_blk, r_band = NA_BLOCK_ROWS, NA_BAND_ROWS
    n_blocks = rows // r_blk
    row_off = np.zeros((3, r_blk, r_band), np.int32)
    row_ok = np.zeros((3, r_blk, r_band), bool)
    for variant, rb in enumerate((0, 1, n_blocks - 1)):
        band0 = min(max(rb - 1, 0), n_blocks - 3) * r_blk
        for qr in range(r_blk):
            r = rb * r_blk + qr
            r0 = min(max(r - NA_WIN_ROWS // 2, 0), rows - NA_WIN_ROWS)
            for kr in range(r_band):
                ka = band0 + kr
                row_ok[variant, qr, kr] = r0 <= ka < r0 + NA_WIN_ROWS
                row_off[variant, qr, kr] = np.clip(ka - r + NA_WIN_ROWS - 1, 0, 2 * NA_WIN_ROWS - 2)
    qc = np.arange(GRID_W)
    c0 = np.clip(qc - NA_WIN_COLS // 2, 0, GRID_W - NA_WIN_COLS)
    kc = np.arange(GRID_W)
    col_ok = (kc[None, :] >= c0[:, None]) & (kc[None, :] < c0[:, None] + NA_WIN_COLS)
    heads, n_ro, _ = rpb_l.shape
    pad = GRID_W - NA_WIN_COLS
    rp = jnp.pad(rpb_l.astype(jnp.float32) * LOG2E, ((0, 0), (0, 0), (pad, pad)))
    tiles = jnp.stack([rp[:, :, GRID_W - 1 - q:2 * GRID_W - 1 - q] for q in range(GRID_W)],
                      axis=2)
    tiles = jnp.where(jnp.asarray(col_ok)[None, None], tiles, MASK_VALUE)
    masked = jnp.full((heads, GRID_W, GRID_W), MASK_VALUE, jnp.float32)
    variants = []
    for variant in range(3):
        block_rows = []
        for qr in range(r_blk):
            block_rows.append(jnp.concatenate(
                [tiles[:, row_off[variant, qr, kr]] if row_ok[variant, qr, kr] else masked
                 for kr in range(r_band)], axis=-1))
        variants.append(jnp.concatenate(block_rows, axis=-2))
    return jnp.stack(variants, axis=1)


def _na_kernel(q_ref, k_ref, v_ref, kc_ref, vc_ref, sg_ref, bias_ref, o_ref):
    seq = q_ref.shape[1]
    tq = NA_BLOCK_ROWS * GRID_W
    band = NA_BAND_ROWS * GRID_W
    n_blocks = seq // tq
    kctx = kc_ref[0]
    vctx = vc_ref[0]
    dims = (((1,), (1,)), ((), ()))

    def body(rb, carry):
        q_off = pl.multiple_of(rb * tq, tq)
        k_off = pl.multiple_of(jnp.clip(rb - 1, 0, n_blocks - 3) * tq, tq)
        variant = jnp.where(rb == 0, 0, jnp.where(rb == n_blocks - 1, 2, 1))
        q = q_ref[0, pl.ds(q_off, tq), :]
        s_w = lax.dot_general(q, k_ref[0, pl.ds(k_off, band), :], dims,
                              preferred_element_type=jnp.float32) + bias_ref[0, variant]
        s_c = lax.dot_general(q, kctx, dims, preferred_element_type=jnp.float32)
        m = jnp.maximum(jnp.max(s_w, axis=1, keepdims=True), jnp.max(s_c, axis=1, keepdims=True))
        p_w = jnp.exp2(s_w - m)
        p_c = jnp.exp2(s_c - m)
        l = jnp.sum(p_w, axis=1, keepdims=True) + jnp.sum(p_c, axis=1, keepdims=True)
        o = (jnp.dot(p_w.astype(jnp.bfloat16), v_ref[0, pl.ds(k_off, band), :],
                     preferred_element_type=jnp.float32)
             + jnp.dot(p_c.astype(jnp.bfloat16), vctx, preferred_element_type=jnp.float32))
        o = o * (1.0 / l)
        o_ref[0, pl.ds(q_off, tq), :] = (
            o * sg_ref[0, pl.ds(q_off, tq), :].astype(jnp.float32)).astype(o_ref.dtype)
        return carry

    lax.fori_loop(0, n_blocks, body, 0, unroll=2)


def _neighbourhood_attention(proj, ctxp, bias):
    b, seq, _ = proj.shape
    n_ctx = ctxp.shape[1]
    hb = NA_HEADS
    tq = NA_BLOCK_ROWS * GRID_W
    band = NA_BAND_ROWS * GRID_W
    return pl.pallas_call(
        _na_kernel,
        grid=(b, NA_HEADS),
        in_specs=[
            pl.BlockSpec((1, seq, LANES), lambda bi, h: (bi, 0, COL_QB * hb + h)),
            pl.BlockSpec((1, seq, LANES), lambda bi, h: (bi, 0, COL_KB * hb + h)),
            pl.BlockSpec((1, seq, LANES), lambda bi, h: (bi, 0, COL_VB * hb + h)),
            pl.BlockSpec((1, n_ctx, LANES), lambda bi, h: (bi, 0, 2 * hb + h)),
            pl.BlockSpec((1, n_ctx, LANES), lambda bi, h: (bi, 0, 3 * hb + h)),
            pl.BlockSpec((1, seq, LANES), lambda bi, h: (bi, 0, COL_GB * hb + h)),
            pl.BlockSpec((1, 3, tq, band), lambda bi, h: (h, 0, 0, 0)),
        ],
        out_specs=pl.BlockSpec((1, seq, LANES), lambda bi, h: (bi, 0, h)),
        out_shape=jax.ShapeDtypeStruct((b, seq, NA_HEADS * NA_HEAD_DIM), jnp.bfloat16),
        compiler_params=pltpu.CompilerParams(
            dimension_semantics=("parallel", "parallel"), vmem_limit_bytes=VMEM_LIMIT_BYTES),
        name="neighbourhood_attention",
    )(proj, proj, proj, ctxp, ctxp, proj, bias)


def _out_kernel(oa_ref, ob_ref, wa_ref, wb_ref, x_ref, gate_ref, fg_ref, o_ref, *, final_norm):
    mixed = (jnp.dot(oa_ref[...], wa_ref[...], preferred_element_type=jnp.float32)
             + jnp.dot(ob_ref[...], wb_ref[...], preferred_element_type=jnp.float32))
    h = x_ref[...] + gate_ref[0] * mixed
    if final_norm:
        ms = jnp.mean(h * h, axis=-1, keepdims=True)
        h = h * lax.rsqrt(ms + NORM_EPS) * fg_ref[...]
    o_ref[...] = h


def _out_projection(oa2, ob2, w_out_bf16, x2, gate, final_g, *, tm, rows_per_mod, final_norm):
    m, d = x2.shape
    wa_rows = oa2.shape[1]
    wb_rows = ob2.shape[1]
    assert wa_rows == wb_rows
    blocks_per_mod = rows_per_mod // tm
    kernel = functools.partial(_out_kernel, final_norm=final_norm)
    return pl.pallas_call(
        kernel,
        grid=(m // tm,),
        in_specs=[
            pl.BlockSpec((tm, wa_rows), lambda i: (i, 0)),
            pl.BlockSpec((tm, wb_rows), lambda i: (i, 0)),
            pl.BlockSpec((wa_rows, d), lambda i: (0, 0)),
            pl.BlockSpec((wb_rows, d), lambda i: (1, 0)),
            pl.BlockSpec((tm, d), lambda i: (i, 0)),
            pl.BlockSpec((1, 1, d), lambda i: (i // blocks_per_mod, 0, 0)),
            pl.BlockSpec((1, d), lambda i: (0, 0)),
        ],
        out_specs=pl.BlockSpec((tm, d), lambda i: (i, 0)),
        out_shape=jax.ShapeDtypeStruct((m, d), jnp.float32),
        compiler_params=pltpu.CompilerParams(
            dimension_semantics=("parallel",), vmem_limit_bytes=VMEM_LIMIT_BYTES),
        name="out_projection",
    )(oa2, ob2, w_out_bf16, w_out_bf16, x2, gate, final_g.reshape(1, d))


def kernel(x, c, ctx, c_ctx, norm_g, w_mod, b_mod, w_in, w_out, lam_q1, lam_k1, lam_q2, lam_k2,
           subln_g, rpb, final_g):
    b, seq, d = x.shape
    n_ctx = ctx.shape[1]
    depth = w_in.shape[0]
    rows = seq // GRID_W
    group = DA_HEADS * DA_V_DIM
    assert w_in.shape[2] == 8 * group and NA_HEADS * NA_HEAD_DIM == group
    assert DA_V_DIM == LANES and NA_HEAD_DIM == LANES

    rope_tabs = _rope_tables(seq)
    q_scales = (DA_QK_DIM ** -0.5 * LOG2E, NA_HEAD_DIM ** -0.5 * LOG2E)
    x_kinds = ("rope_q", "rope_k", "plain", "silu", "scale_qb", "plain", "plain", "silu")
    ctx_cols = (COL_KA, COL_VA, COL_KB, COL_VB)

    mod_rows = 8
    cc = jnp.zeros((mod_rows, d), jnp.float32).at[:b].set(c).at[b].set(c_ctx)
    ctx2 = ctx.reshape(b * n_ctx, d)

    h2 = x.reshape(b * seq, d)
    for layer in range(depth):
        lambda_init = 0.8 - 0.6 * math.exp(-0.3 * layer)
        mod = _modulation(cc, w_mod[layer], b_mod[layer])
        shift = mod[:b, :d].reshape(b, 1, d)
        scale = mod[:b, d:2 * d].reshape(b, 1, d)
        gate = mod[:b, 2 * d:].reshape(b, 1, d)
        shift_c = mod[b:b + 1, :d].reshape(1, 1, d)
        scale_c = mod[b:b + 1, d:2 * d].reshape(1, 1, d)

        wi = w_in[layer].astype(jnp.bfloat16)
        proj = _in_projection(
            h2, norm_g[layer], scale, shift, wi, tm=1024, tn=group, rows_per_mod=seq,
            col_blocks=tuple(range(8)), kinds=x_kinds, q_scales=q_scales, rope_tabs=rope_tabs,
            name="in_projection").reshape(b, seq, 8 * group)
        ctxp = _in_projection(
            ctx2, norm_g[layer], scale_c, shift_c, wi, tm=b * n_ctx, tn=group,
            rows_per_mod=b * n_ctx, col_blocks=ctx_cols, kinds=("plain",) * 4, q_scales=q_scales,
            rope_tabs=None, name="ctx_projection").reshape(b, n_ctx, 4 * group)

        lam_p = jnp.stack([lam_q1[layer], lam_k1[layer], lam_q2[layer], lam_k2[layer]]
                          ).astype(jnp.float32)
        oa = _diff_attention(proj, ctxp, lam_p, subln_g[layer], lambda_init=lambda_init,
                             tq=512, tkb=512)
        ob = _neighbourhood_attention(proj, ctxp, _na_bias_table(rpb[layer], rows))

        h2 = _out_projection(
            oa.reshape(b * seq, group), ob.reshape(b * seq, group),
            w_out[layer].astype(jnp.bfloat16), h2, gate, final_g, tm=512, rows_per_mod=seq,
            final_norm=(layer == depth - 1))
    return h2.reshape(b, seq, d)
```

```python
import functools
import math

import numpy as np
import jax
import jax.numpy as jnp
from jax import lax
from jax.experimental import pallas as pl
from jax.experimental.pallas import tpu as pltpu

GRID_W = 64
DA_HEADS = 8
DA_QK_DIM = 64
DA_V_DIM = 2 * DA_QK_DIM
NA_HEADS = 8
NA_HEAD_DIM = 128
NA_WIN_ROWS = 8
NA_WIN_COLS = 16
ROPE_BASE = 10000.0
NORM_EPS = 1e-6
SUBLN_EPS = 1e-5

LANES = 128
VMEM_LIMIT_BYTES = 56 * 1024 * 1024
LOG2E = math.log2(math.e)
MASK_VALUE = -1e30

COL_QA, COL_KA, COL_VA, COL_GA, COL_QB, COL_KB, COL_VB, COL_GB = range(8)


def _silu(x):
    return x * (1.0 / (1.0 + jnp.exp(-x)))


def _mod_kernel(c_ref, w_ref, b_ref, o_ref):
    a = _silu(c_ref[...]).astype(jnp.bfloat16)
    w = w_ref[...].astype(jnp.bfloat16)
    o_ref[...] = jnp.dot(a, w, preferred_element_type=jnp.float32) + b_ref[...]


def _modulation(cc, w_mod, b_mod):
    rows, d = cc.shape
    n = w_mod.shape[1]
    tn = 768
    return pl.pallas_call(
        _mod_kernel,
        grid=(n // tn,),
        in_specs=[
            pl.BlockSpec((rows, d), lambda j: (0, 0)),
            pl.BlockSpec((d, tn), lambda j: (0, j)),
            pl.BlockSpec((1, tn), lambda j: (0, j)),
        ],
        out_specs=pl.BlockSpec((rows, tn), lambda j: (0, j)),
        out_shape=jax.ShapeDtypeStruct((rows, n), jnp.float32),
        compiler_params=pltpu.CompilerParams(
            dimension_semantics=("parallel",), vmem_limit_bytes=VMEM_LIMIT_BYTES),
        name="modulation",
    )(cc, w_mod, b_mod.reshape(1, n))


def _rope_tables(seq):
    t = jnp.arange(seq, dtype=jnp.int32)
    row_pos = (t // GRID_W).astype(jnp.float32)
    col_pos = (t % GRID_W).astype(jnp.float32)
    half = DA_QK_DIM // 4
    inv_freq = ROPE_BASE ** (-jnp.arange(half, dtype=jnp.float32) / half)
    lane = np.arange(LANES)
    d = lane % DA_QK_DIM
    use_col = (d // (DA_QK_DIM // 2)) == 1
    f = d % half
    first_half = (d % (2 * half)) < half
    pos = jnp.where(jnp.asarray(use_col)[None, :], col_pos[:, None], row_pos[:, None])
    ang = pos * inv_freq[jnp.asarray(f)][None, :]
    cos = jnp.cos(ang)
    sin = jnp.sin(ang)
    sin_signed = jnp.where(jnp.asarray(first_half)[None, :], -sin, sin)
    return cos, sin_signed


def _proj_kernel(*refs, kinds, q_scales, with_rope):
    if with_rope:
        x_ref, g_ref, scale_ref, shift_ref, w_ref, cos_ref, sin_ref, o_ref, hx_ref = refs
    else:
        x_ref, g_ref, scale_ref, shift_ref, w_ref, o_ref, hx_ref = refs
    j = pl.program_id(1)

    @pl.when(j == 0)
    def _():
        x = x_ref[...]
        ms = jnp.mean(x * x, axis=-1, keepdims=True)
        y = x * lax.rsqrt(ms + NORM_EPS) * g_ref[...]
        hx = y * (1.0 + scale_ref[0]) + shift_ref[0]
        hx_ref[...] = hx.astype(jnp.bfloat16)

    def matmul():
        return jnp.dot(hx_ref[...], w_ref[...], preferred_element_type=jnp.float32)

    if all(kind == "plain" for kind in kinds):
        o_ref[...] = matmul().astype(o_ref.dtype)
        return

    def columns(names):
        return [jj for jj, kind in enumerate(kinds) if kind in names]

    def any_of(cols):
        return functools.reduce(jnp.logical_or, [j == jj for jj in cols])

    def col_scale():
        s = jnp.float32(1.0)
        for jj, kind in enumerate(kinds):
            if kind == "rope_q":
                s = jnp.where(j == jj, jnp.float32(q_scales[0]), s)
            elif kind == "scale_qb":
                s = jnp.where(j == jj, jnp.float32(q_scales[1]), s)
        return s

    @pl.when(any_of(columns(("rope_q", "rope_k"))))
    def _():
        acc = matmul()
        s = col_scale()
        a = cos_ref[...] * s
        b = sin_ref[...] * s
        lane = lax.broadcasted_iota(jnp.int32, a.shape, 1)
        half = DA_QK_DIM // 4
        first_half = (lane % (2 * half)) < half
        for hh in range(acc.shape[1] // LANES):
            xs = acc[:, hh * LANES:(hh + 1) * LANES]
            fwd = pltpu.roll(xs, LANES - half, 1)
            bwd = pltpu.roll(xs, half, 1)
            y = xs * a + jnp.where(first_half, fwd, bwd) * b
            o_ref[:, hh * LANES:(hh + 1) * LANES] = y.astype(o_ref.dtype)

    @pl.when(any_of(columns(("silu",))))
    def _():
        o_ref[...] = _silu(matmul()).astype(o_ref.dtype)

    @pl.when(any_of(columns(("plain", "scale_qb"))))
    def _():
        o_ref[...] = (matmul() * col_scale()).astype(o_ref.dtype)


def _in_projection(x2, norm_g, scale, shift, w_bf16, *, tm, tn, rows_per_mod, col_blocks,
                   kinds, q_scales, rope_tabs, name):
    m, d = x2.shape
    nj = len(col_blocks)
    with_rope = rope_tabs is not None
    blocks_per_mod = rows_per_mod // tm
    start, skip = col_blocks[0], 0
    if nj > 2:
        skip = col_blocks[2] - col_blocks[1] - 1
    assert all(col_blocks[jj] == start + jj + skip * (jj // 2) for jj in range(nj))

    def w_map(i, j):
        return (0, start + j + skip * (j // 2))

    in_specs = [
        pl.BlockSpec((tm, d), lambda i, j: (i, 0)),
        pl.BlockSpec((1, d), lambda i, j: (0, 0)),
        pl.BlockSpec((1, 1, d), lambda i, j: (i // blocks_per_mod, 0, 0)),
        pl.BlockSpec((1, 1, d), lambda i, j: (i // blocks_per_mod, 0, 0)),
        pl.BlockSpec((d, tn), w_map),
    ]
    args = [x2, norm_g.reshape(1, d), scale, shift, w_bf16]
    if with_rope:
        seq_blocks = rope_tabs[0].shape[0] // tm
        in_specs += [pl.BlockSpec((tm, LANES), lambda i, j: (i % seq_blocks, 0))] * 2
        args += list(rope_tabs)
    kernel = functools.partial(_proj_kernel, kinds=tuple(kinds), q_scales=q_scales,
                               with_rope=with_rope)
    return pl.pallas_call(
        kernel,
        grid=(m // tm, nj),
        in_specs=in_specs,
        out_specs=pl.BlockSpec((tm, tn), lambda i, j: (i, j)),
        out_shape=jax.ShapeDtypeStruct((m, nj * tn), jnp.bfloat16),
        scratch_shapes=[pltpu.VMEM((tm, d), jnp.bfloat16)],
        compiler_params=pltpu.CompilerParams(
            dimension_semantics=("parallel", "arbitrary"), vmem_limit_bytes=VMEM_LIMIT_BYTES),
        name=name,
    )(*args)


def _diff_attn_kernel(q_ref, k_ref, v_ref, kc_ref, vc_ref, sg_ref, lam_ref, subg_ref, o_ref,
                      s_ref, cm_ref, m_ref, acc_ref, vt_ref, vct_ref, *, tkb, lambda_init):
    seq = k_ref.shape[1]
    tq = s_ref.shape[3]
    nkv = seq // tkb
    nq = seq // tq
    dims = (((1,), (1,)), ((), ()))

    dv = v_ref.shape[2]
    for c in range(nkv):
        vt_ref[:dv, c * tkb:(c + 1) * tkb] = v_ref[0, c * tkb:(c + 1) * tkb, :].T
    vt_ref[dv:, :] = jnp.ones((vt_ref.shape[0] - dv, seq), vt_ref.dtype)
    vct_ref[:dv, :] = vc_ref[0].T
    vct_ref[dv:, :] = jnp.ones((vct_ref.shape[0] - dv, vct_ref.shape[1]), vct_ref.dtype)

    lam_p = lam_ref[...]
    lam = (jnp.exp(jnp.sum(lam_p[0:1] * lam_p[1:2], axis=1, keepdims=True))
           - jnp.exp(jnp.sum(lam_p[2:3] * lam_p[3:4], axis=1, keepdims=True)) + lambda_init)

    def masked_q(qi):
        q = q_ref[0, pl.ds(pl.multiple_of(qi * tq, tq), tq), :]
        lane = lax.broadcasted_iota(jnp.int32, q.shape, 1)
        zero = jnp.zeros_like(q)
        return (jnp.where(lane < DA_QK_DIM, q, zero), jnp.where(lane >= DA_QK_DIM, q, zero))

    def scores_t(kblk, qmi):
        return lax.dot_general(kblk, qmi, dims, preferred_element_type=jnp.float32)

    def scores(slot, jb, qm):
        off = pl.multiple_of(jb * tkb, tkb)
        kblk = k_ref[0, pl.ds(off, tkb), :]
        for i in range(2):
            st = scores_t(kblk, qm[i])
            s_ref[slot, i] = st
            cm_ref[slot, i] = jnp.max(st, axis=0, keepdims=True)

    def online_update(i, st, cm, vt_blk):
        m_prev = m_ref[i]
        m_new = jnp.maximum(m_prev, cm)
        alpha = jnp.exp2(m_prev - m_new)
        p = jnp.exp2(st - m_new)
        acc_ref[i] = alpha * acc_ref[i] + jnp.dot(
            vt_blk, p.astype(jnp.bfloat16), preferred_element_type=jnp.float32)
        m_ref[i] = m_new

    def consume(slot, jb):
        off = pl.multiple_of(jb * tkb, tkb)
        vt_blk = vt_ref[:, pl.ds(off, tkb)]
        for i in range(2):
            online_update(i, s_ref[slot, i], cm_ref[slot, i], vt_blk)

    scores(0, 0, masked_q(0))

    def q_block(qi, carry):
        qm = masked_q(qi)
        m_ref[...] = jnp.full(m_ref.shape, -jnp.inf, jnp.float32)
        acc_ref[...] = jnp.zeros(acc_ref.shape, jnp.float32)

        for jb in range(nkv - 1):
            scores((jb + 1) % 2, jb + 1, qm)
            consume(jb % 2, jb)
        kctx = kc_ref[0]
        s_ctx = [scores_t(kctx, qm[i]) for i in range(2)]
        scores(0, 0, masked_q(jnp.minimum(qi + 1, nq - 1)))
        consume((nkv - 1) % 2, nkv - 1)
        for i in range(2):
            online_update(i, s_ctx[i], jnp.max(s_ctx[i], axis=0, keepdims=True), vct_ref[...])

        attn_t = (acc_ref[0, :dv] * (1.0 / acc_ref[0, dv:dv + 1])
                  - lam * (acc_ref[1, :dv] * (1.0 / acc_ref[1, dv:dv + 1])))
        attn = attn_t.T
        ms = jnp.mean(attn * attn, axis=-1, keepdims=True)
        y = attn * lax.rsqrt(ms + SUBLN_EPS) * subg_ref[...]
        y = y * (1.0 - lambda_init)
        rows = pl.ds(pl.multiple_of(qi * tq, tq), tq)
        o_ref[0, rows, :] = (y * sg_ref[0, rows, :].astype(jnp.float32)).astype(o_ref.dtype)
        return carry

    lax.fori_loop(0, nq, q_block, 0)


def _diff_attention(proj, ctxp, lam_p, subln_g, *, lambda_init, tq, tkb):
    b, seq, _ = proj.shape
    n_ctx = ctxp.shape[1]
    hb = DA_HEADS
    assert seq % (2 * tkb) == 0 and seq // tkb >= 2
    ones_rows = 16
    kernel = functools.partial(_diff_attn_kernel, tkb=tkb, lambda_init=lambda_init)
    return pl.pallas_call(
        kernel,
        grid=(b, DA_HEADS),
        in_specs=[
            pl.BlockSpec((1, seq, LANES), lambda bi, h: (bi, 0, COL_QA * hb + h)),
            pl.BlockSpec((1, seq, LANES), lambda bi, h: (bi, 0, COL_KA * hb + h)),
            pl.BlockSpec((1, seq, LANES), lambda bi, h: (bi, 0, COL_VA * hb + h)),
            pl.BlockSpec((1, n_ctx, LANES), lambda bi, h: (bi, 0, 0 * hb + h)),
            pl.BlockSpec((1, n_ctx, LANES), lambda bi, h: (bi, 0, 1 * hb + h)),
            pl.BlockSpec((1, seq, LANES), lambda bi, h: (bi, 0, COL_GA * hb + h)),
            pl.BlockSpec(lam_p.shape, lambda bi, h: (0, 0)),
            pl.BlockSpec((1, LANES), lambda bi, h: (0, 0)),
        ],
        out_specs=pl.BlockSpec((1, seq, LANES), lambda bi, h: (bi, 0, h)),
        out_shape=jax.ShapeDtypeStruct((b, seq, DA_HEADS * DA_V_DIM), jnp.bfloat16),
        scratch_shapes=[
            pltpu.VMEM((2, 2, tkb, tq), jnp.float32),
            pltpu.VMEM((2, 2, 1, tq), jnp.float32),
            pltpu.VMEM((2, 1, tq), jnp.float32),
            pltpu.VMEM((2, DA_V_DIM + ones_rows, tq), jnp.float32),
            pltpu.VMEM((DA_V_DIM + ones_rows, seq), jnp.bfloat16),
            pltpu.VMEM((DA_V_DIM + ones_rows, n_ctx), jnp.bfloat16),
        ],
        compiler_params=pltpu.CompilerParams(
            dimension_semantics=("parallel", "parallel"), vmem_limit_bytes=VMEM_LIMIT_BYTES),
        name="diff_attention",
    )(proj, proj, proj, ctxp, ctxp, proj, lam_p, subln_g.reshape(1, LANES))


NA_BLOCK_ROWS = NA_WIN_ROWS // 2
NA_BAND_ROWS = 3 * NA_BLOCK_ROWS


def _na_bias_table(rpb_l, rows):
    r_blk, r_band = NA_BLOCK_ROWS, NA_BAND_ROWS
    n_blocks = rows // r_blk
    row_off = np.zeros((3, r_blk, r_band), np.int32)
    row_ok = np.zeros((3, r_blk, r_band), bool)
    for variant, rb in enumerate((0, 1, n_blocks - 1)):
        band0 = min(max(rb - 1, 0), n_blocks - 3) * r_blk
        for qr in range(r_blk):
            r = rb * r_blk + qr
            r0 = min(max(r - NA_WIN_ROWS // 2, 0), rows - NA_WIN_ROWS)
            for kr in range(r_band):
                ka = band0 + kr
                row_ok[variant, qr, kr] = r0 <= ka < r0 + NA_WIN_ROWS
                row_off[variant, qr, kr] = np.clip(ka - r + NA_WIN_ROWS - 1, 0, 2 * NA_WIN_ROWS - 2)
    qc = np.arange(GRID_W)
    c0 = np.clip(qc - NA_WIN_COLS // 2, 0, GRID_W - NA_WIN_COLS)
    kc = np.arange(GRID_W)
    col_ok = (kc[None, :] >= c0[:, None]) & (kc[None, :] < c0[:, None] + NA_WIN_COLS)
    heads, n_ro, _ = rpb_l.shape
    pad = GRID_W - NA_WIN_COLS
    rp = jnp.pad(rpb_l.astype(jnp.float32) * LOG2E, ((0, 0), (0, 0), (pad, pad)))
    tiles = jnp.stack([rp[:, :, GRID_W - 1 - q:2 * GRID_W - 1 - q] for q in range(GRID_W)],
                      axis=2)
    tiles = jnp.where(jnp.asarray(col_ok)[None, None], tiles, MASK_VALUE)
    masked = jnp.full((heads, GRID_W, GRID_W), MASK_VALUE, jnp.float32)
    variants = []
    for variant in range(3):
        block_rows = []
        for qr in range(r_blk):
            block_rows.append(jnp.concatenate(
                [tiles[:, row_off[variant, qr, kr]] if row_ok[variant, qr, kr] else masked
                 for kr in range(r_band)], axis=-1))
        variants.append(jnp.concatenate(block_rows, axis=-2))
    return jnp.stack(variants, axis=1)


def _na_kernel(q_ref, k_ref, v_ref, kc_ref, vc_ref, sg_ref, bias_ref, o_ref,
               s_ref, v1_ref, vc1_ref):
    seq = q_ref.shape[1]
    dv = v_ref.shape[2]
    n_ctx = kc_ref.shape[1]
    tq = NA_BLOCK_ROWS * GRID_W
    band = NA_BAND_ROWS * GRID_W
    n_blocks = seq // tq
    kctx = kc_ref[0]
    dims = (((1,), (1,)), ((), ()))

    v1_ref[:, :dv] = v_ref[0]
    v1_ref[:, dv:] = jnp.ones((seq, dv), v1_ref.dtype)
    vc1_ref[:, :dv] = vc_ref[0]
    vc1_ref[:, dv:] = jnp.ones((n_ctx, dv), vc1_ref.dtype)

    def band_start(rb):
        return pl.multiple_of(jnp.clip(rb - 1, 0, n_blocks - 3) * tq, tq)

    def scores(slot, rb):
        variant = jnp.where(rb == 0, 0, jnp.where(rb == n_blocks - 1, 2, 1))
        q = q_ref[0, pl.ds(pl.multiple_of(rb * tq, tq), tq), :]
        s_ref[slot, :, :band] = lax.dot_general(
            q, k_ref[0, pl.ds(band_start(rb), band), :], dims,
            preferred_element_type=jnp.float32) + bias_ref[0, variant]
        s_ref[slot, :, band:] = lax.dot_general(q, kctx, dims,
                                                preferred_element_type=jnp.float32)

    def consume(slot, rb):
        s = s_ref[slot]
        p = jnp.exp2(s - jnp.max(s, axis=1, keepdims=True)).astype(jnp.bfloat16)
        o = (jnp.dot(p[:, :band], v1_ref[pl.ds(band_start(rb), band), :],
                     preferred_element_type=jnp.float32)
             + jnp.dot(p[:, band:], vc1_ref[...], preferred_element_type=jnp.float32))
        o = o[:, :dv] * (1.0 / o[:, dv:])
        rows = pl.ds(pl.multiple_of(rb * tq, tq), tq)
        o_ref[0, rows, :] = (o * sg_ref[0, rows, :].astype(jnp.float32)).astype(o_ref.dtype)

    scores(0, 0)

    def body(rr, carry):
        rb = 2 * rr
        scores(1, rb + 1)
        consume(0, rb)
        scores(0, jnp.minimum(rb + 2, n_blocks - 1))
        consume(1, rb + 1)
        return carry

    lax.fori_loop(0, n_blocks // 2, body, 0, unroll=2)


def _neighbourhood_attention(proj, ctxp, bias):
    b, seq, _ = proj.shape
    n_ctx = ctxp.shape[1]
    hb = NA_HEADS
    tq = NA_BLOCK_ROWS * GRID_W
    band = NA_BAND_ROWS * GRID_W
    return pl.pallas_call(
        _na_kernel,
        grid=(b, NA_HEADS),
        in_specs=[
            pl.BlockSpec((1, seq, LANES), lambda bi, h: (bi, 0, COL_QB * hb + h)),
            pl.BlockSpec((1, seq, LANES), lambda bi, h: (bi, 0, COL_KB * hb + h)),
            pl.BlockSpec((1, seq, LANES), lambda bi, h: (bi, 0, COL_VB * hb + h)),
            pl.BlockSpec((1, n_ctx, LANES), lambda bi, h: (bi, 0, 2 * hb + h)),
            pl.BlockSpec((1, n_ctx, LANES), lambda bi, h: (bi, 0, 3 * hb + h)),
            pl.BlockSpec((1, seq, LANES), lambda bi, h: (bi, 0, COL_GB * hb + h)),
            pl.BlockSpec((1, 3, tq, band), lambda bi, h: (h, 0, 0, 0)),
        ],
        out_specs=pl.BlockSpec((1, seq, LANES), lambda bi, h: (bi, 0, h)),
        out_shape=jax.ShapeDtypeStruct((b, seq, NA_HEADS * NA_HEAD_DIM), jnp.bfloat16),
        scratch_shapes=[
            pltpu.VMEM((2, tq, band + n_ctx), jnp.float32),
            pltpu.VMEM((seq, 2 * NA_HEAD_DIM), jnp.bfloat16),
            pltpu.VMEM((n_ctx, 2 * NA_HEAD_DIM), jnp.bfloat16),
        ],
        compiler_params=pltpu.CompilerParams(
            dimension_semantics=("parallel", "parallel"), vmem_limit_bytes=VMEM_LIMIT_BYTES),
        name="neighbourhood_attention",
    )(proj, proj, proj, ctxp, ctxp, proj, bias)


def _out_kernel(oa_ref, ob_ref, wa_ref, wb_ref, x_ref, gate_ref, fg_ref, o_ref, *, final_norm):
    mixed = (jnp.dot(oa_ref[...], wa_ref[...], preferred_element_type=jnp.float32)
             + jnp.dot(ob_ref[...], wb_ref[...], preferred_element_type=jnp.float32))
    h = x_ref[...] + gate_ref[0] * mixed
    if final_norm:
        ms = jnp.mean(h * h, axis=-1, keepdims=True)
        h = h * lax.rsqrt(ms + NORM_EPS) * fg_ref[...]
    o_ref[...] = h


def _out_projection(oa2, ob2, w_out_bf16, x2, gate, final_g, *, tm, rows_per_mod, final_norm):
    m, d = x2.shape
    wa_rows = oa2.shape[1]
    wb_rows = ob2.shape[1]
    assert wa_rows == wb_rows
    blocks_per_mod = rows_per_mod // tm
    kernel = functools.partial(_out_kernel, final_norm=final_norm)
    return pl.pallas_call(
        kernel,
        grid=(m // tm,),
        in_specs=[
            pl.BlockSpec((tm, wa_rows), lambda i: (i, 0)),
            pl.BlockSpec((tm, wb_rows), lambda i: (i, 0)),
            pl.BlockSpec((wa_rows, d), lambda i: (0, 0)),
            pl.BlockSpec((wb_rows, d), lambda i: (1, 0)),
            pl.BlockSpec((tm, d), lambda i: (i, 0)),
            pl.BlockSpec((1, 1, d), lambda i: (i // blocks_per_mod, 0, 0)),
            pl.BlockSpec((1, d), lambda i: (0, 0)),
        ],
        out_specs=pl.BlockSpec((tm, d), lambda i: (i, 0)),
        out_shape=jax.ShapeDtypeStruct((m, d), jnp.float32),
        compiler_params=pltpu.CompilerParams(
            dimension_semantics=("parallel",), vmem_limit_bytes=VMEM_LIMIT_BYTES),
        name="out_projection",
    )(oa2, ob2, w_out_bf16, w_out_bf16, x2, gate, final_g.reshape(1, d))


def kernel(x, c, ctx, c_ctx, norm_g, w_mod, b_mod, w_in, w_out, lam_q1, lam_k1, lam_q2, lam_k2,
           subln_g, rpb, final_g):
    b, seq, d = x.shape
    n_ctx = ctx.shape[1]
    depth = w_in.shape[0]
    rows = seq // GRID_W
    group = DA_HEADS * DA_V_DIM
    assert w_in.shape[2] == 8 * group and NA_HEADS * NA_HEAD_DIM == group
    assert DA_V_DIM == LANES and NA_HEAD_DIM == LANES

    rope_tabs = _rope_tables(seq)
    q_scales = (DA_QK_DIM ** -0.5 * LOG2E, NA_HEAD_DIM ** -0.5 * LOG2E)
    x_kinds = ("rope_q", "rope_k", "plain", "silu", "scale_qb", "plain", "plain", "silu")
    ctx_cols = (COL_KA, COL_VA, COL_KB, COL_VB)

    mod_rows = 8
    cc = jnp.zeros((mod_rows, d), jnp.float32).at[:b].set(c).at[b].set(c_ctx)
    ctx2 = ctx.reshape(b * n_ctx, d)

    h2 = x.reshape(b * seq, d)
    for layer in range(depth):
        lambda_init = 0.8 - 0.6 * math.exp(-0.3 * layer)
        mod = _modulation(cc, w_mod[layer], b_mod[layer])
        shift = mod[:b, :d].reshape(b, 1, d)
        scale = mod[:b, d:2 * d].reshape(b, 1, d)
        gate = mod[:b, 2 * d:].reshape(b, 1, d)
        shift_c = mod[b:b + 1, :d].reshape(1, 1, d)
        scale_c = mod[b:b + 1, d:2 * d].reshape(1, 1, d)

        wi = w_in[layer].astype(jnp.bfloat16)
        proj = _in_projection(
            h2, norm_g[layer], scale, shift, wi, tm=1024, tn=group, rows_per_mod=seq,
            col_blocks=tuple(range(8)), kinds=x_kinds, q_scales=q_scales, rope_tabs=rope_tabs,
            name="in_projection").reshape(b, seq, 8 * group)
        ctxp = _in_projection(
            ctx2, norm_g[layer], scale_c, shift_c, wi, tm=b * n_ctx, tn=group,
            rows_per_mod=b * n_ctx, col_blocks=ctx_cols, kinds=("plain",) * 4, q_scales=q_scales,
            rope_tabs=None, name="ctx_projection").reshape(b, n_ctx, 4 * group)

        lam_p = jnp.stack([lam_q1[layer], lam_k1[layer], lam_q2[layer], lam_k2[layer]]
                          ).astype(jnp.float32)
        oa = _diff_attention(proj, ctxp, lam_p, subln_g[layer], lambda_init=lambda_init,
                             tq=512, tkb=512)
        ob = _neighbourhood_attention(proj, ctxp, _na_bias_table(rpb[layer], rows))

        h2 = _out_projection(
            oa.reshape(b * seq, group), ob.reshape(b * seq, group),
            w_out[layer].astype(jnp.bfloat16), h2, gate, final_g, tm=512, rows_per_mod=seq,
            final_norm=(layer == depth - 1))
    return h2.reshape(b, seq, d)
```

```python
import functools
import math

import numpy as np
import jax
import jax.numpy as jnp
from jax import lax
from jax.experimental import pallas as pl
from jax.experimental.pallas import tpu as pltpu

GRID_W = 64
DA_HEADS = 8
DA_QK_DIM = 64
DA_V_DIM = 2 * DA_QK_DIM
NA_HEADS = 8
NA_HEAD_DIM = 128
NA_WIN_ROWS = 8
NA_WIN_COLS = 16
ROPE_BASE = 10000.0
NORM_EPS = 1e-6
SUBLN_EPS = 1e-5

LANES = 128
VMEM_LIMIT_BYTES = 56 * 1024 * 1024
LOG2E = math.log2(math.e)
MASK_VALUE = -1e30

COL_QA, COL_KA, COL_VA, COL_GA, COL_QB, COL_KB, COL_VB, COL_GB = range(8)


def _silu(x):
    return x * (1.0 / (1.0 + jnp.exp(-x)))


def _mod_kernel(c_ref, w_ref, b_ref, o_ref):
    a = _silu(c_ref[...]).astype(jnp.bfloat16)
    w = w_ref[...].astype(jnp.bfloat16)
    o_ref[...] = jnp.dot(a, w, preferred_element_type=jnp.float32) + b_ref[...]


def _modulation(cc, w_mod, b_mod):
    rows, d = cc.shape
    n = w_mod.shape[1]
    tn = 768
    return pl.pallas_call(
        _mod_kernel,
        grid=(n // tn,),
        in_specs=[
            pl.BlockSpec((rows, d), lambda j: (0, 0)),
            pl.BlockSpec((d, tn), lambda j: (0, j)),
            pl.BlockSpec((1, tn), lambda j: (0, j)),
        ],
        out_specs=pl.BlockSpec((rows, tn), lambda j: (0, j)),
        out_shape=jax.ShapeDtypeStruct((rows, n), jnp.float32),
        compiler_params=pltpu.CompilerParams(
            dimension_semantics=("parallel",), vmem_limit_bytes=VMEM_LIMIT_BYTES),
        name="modulation",
    )(cc, w_mod, b_mod.reshape(1, n))


def _rope_tables(seq):
    t = np.arange(seq)
    row_pos = (t // GRID_W).astype(np.float32)
    col_pos = (t % GRID_W).astype(np.float32)
    half = DA_QK_DIM // 4
    inv_freq = (np.float32(ROPE_BASE) ** (-np.arange(half, dtype=np.float32) / np.float32(half))
                ).astype(np.float32)
    lane = np.arange(LANES)
    d = lane % DA_QK_DIM
    use_col = (d // (DA_QK_DIM // 2)) == 1
    f = d % half
    first_half = (d % (2 * half)) < half
    pos = np.where(use_col[None, :], col_pos[:, None], row_pos[:, None])
    ang = (pos * inv_freq[f][None, :]).astype(np.float32)
    cos = np.cos(ang.astype(np.float64)).astype(np.float32)
    sin = np.sin(ang.astype(np.float64)).astype(np.float32)
    sin_signed = np.where(first_half[None, :], -sin, sin)
    return jnp.asarray(cos), jnp.asarray(sin_signed)


def _proj_kernel(*refs, kinds, q_scales, with_rope):
    if with_rope:
        x_ref, g_ref, scale_ref, shift_ref, w_ref, cos_ref, sin_ref, o_ref, hx_ref = refs
    else:
        x_ref, g_ref, scale_ref, shift_ref, w_ref, o_ref, hx_ref = refs
    j = pl.program_id(1)

    def normed_input():
        x = x_ref[...]
        ms = jnp.mean(x * x, axis=-1, keepdims=True)
        y = x * lax.rsqrt(ms + NORM_EPS) * g_ref[...]
        hx = (y * (1.0 + scale_ref[0]) + shift_ref[0]).astype(jnp.bfloat16)
        hx_ref[...] = hx
        return hx

    def rope_store(acc, s):
        a = cos_ref[...]
        b = sin_ref[...]
        if s != 1.0:
            a = a * s
            b = b * s
        lane = lax.broadcasted_iota(jnp.int32, a.shape, 1)
        half = DA_QK_DIM // 4
        first_half = (lane % (2 * half)) < half
        for hh in range(acc.shape[1] // LANES):
            xs = acc[:, hh * LANES:(hh + 1) * LANES]
            fwd = pltpu.roll(xs, LANES - half, 1)
            bwd = pltpu.roll(xs, half, 1)
            y = xs * a + jnp.where(first_half, fwd, bwd) * b
            o_ref[:, hh * LANES:(hh + 1) * LANES] = y.astype(o_ref.dtype)

    def epilogue(kind, acc):
        if kind == "rope_q":
            rope_store(acc, q_scales[0])
        elif kind == "rope_k":
            rope_store(acc, 1.0)
        elif kind == "silu":
            o_ref[...] = _silu(acc).astype(o_ref.dtype)
        elif kind == "scale_qb":
            o_ref[...] = (acc * q_scales[1]).astype(o_ref.dtype)
        else:
            o_ref[...] = acc.astype(o_ref.dtype)

    @pl.when(j == 0)
    def _():
        hx = normed_input()
        epilogue(kinds[0], jnp.dot(hx, w_ref[...], preferred_element_type=jnp.float32))

    for kind in sorted(set(kinds[1:])):
        cols = [jj for jj in range(1, len(kinds)) if kinds[jj] == kind]
        cond = functools.reduce(jnp.logical_or, [j == jj for jj in cols])

        @pl.when(cond)
        def _(kind=kind):
            epilogue(kind, jnp.dot(hx_ref[...], w_ref[...], preferred_element_type=jnp.float32))


def _in_projection(x2, norm_g, scale, shift, w_bf16, *, tm, tn, rows_per_mod, col_blocks,
                   kinds, q_scales, rope_tabs, name):
    m, d = x2.shape
    nj = len(col_blocks)
    with_rope = rope_tabs is not None
    blocks_per_mod = rows_per_mod // tm
    start, skip = col_blocks[0], 0
    if nj > 2:
        skip = col_blocks[2] - col_blocks[1] - 1
    assert all(col_blocks[jj] == start + jj + skip * (jj // 2) for jj in range(nj))

    def w_map(i, j):
        return (0, start + j + skip * (j // 2))

    in_specs = [
        pl.BlockSpec((tm, d), lambda i, j: (i, 0)),
        pl.BlockSpec((1, d), lambda i, j: (0, 0)),
        pl.BlockSpec((1, 1, d), lambda i, j: (i // blocks_per_mod, 0, 0)),
        pl.BlockSpec((1, 1, d), lambda i, j: (i // blocks_per_mod, 0, 0)),
        pl.BlockSpec((d, tn), w_map),
    ]
    args = [x2, norm_g.reshape(1, d), scale, shift, w_bf16]
    if with_rope:
        seq_blocks = rope_tabs[0].shape[0] // tm
        in_specs += [pl.BlockSpec((tm, LANES), lambda i, j: (i % seq_blocks, 0))] * 2
        args += list(rope_tabs)
    kernel = functools.partial(_proj_kernel, kinds=tuple(kinds), q_scales=q_scales,
                               with_rope=with_rope)
    return pl.pallas_call(
        kernel,
        grid=(m // tm, nj),
        in_specs=in_specs,
        out_specs=pl.BlockSpec((tm, tn), lambda i, j: (i, j)),
        out_shape=jax.ShapeDtypeStruct((m, nj * tn), jnp.bfloat16),
        scratch_shapes=[pltpu.VMEM((tm, d), jnp.bfloat16)],
        compiler_params=pltpu.CompilerParams(
            dimension_semantics=("parallel", "arbitrary"), vmem_limit_bytes=VMEM_LIMIT_BYTES),
        name=name,
    )(*args)


def _diff_attn_kernel(q_ref, k_ref, v_ref, kc_ref, vc_ref, sg_ref, lam_ref, subg_ref, o_ref,
                      s_ref, cm_ref, m_ref, acc_ref, vt_ref, vct_ref, *, tkb, lambda_init):
    seq = k_ref.shape[1]
    tq = s_ref.shape[3]
    nkv = seq // tkb
    nq = seq // tq
    dims = (((1,), (1,)), ((), ()))

    dv = v_ref.shape[2]
    for c in range(nkv):
        vt_ref[:dv, c * tkb:(c + 1) * tkb] = v_ref[0, c * tkb:(c + 1) * tkb, :].T
    vt_ref[dv:, :] = jnp.ones((vt_ref.shape[0] - dv, seq), vt_ref.dtype)
    vct_ref[:dv, :] = vc_ref[0].T
    vct_ref[dv:, :] = jnp.ones((vct_ref.shape[0] - dv, vct_ref.shape[1]), vct_ref.dtype)

    lam_p = lam_ref[...]
    lam = (jnp.exp(jnp.sum(lam_p[0:1] * lam_p[1:2], axis=1, keepdims=True))
           - jnp.exp(jnp.sum(lam_p[2:3] * lam_p[3:4], axis=1, keepdims=True)) + lambda_init)

    def masked_q(qi):
        q = q_ref[0, pl.ds(pl.multiple_of(qi * tq, tq), tq), :]
        lane = lax.broadcasted_iota(jnp.int32, q.shape, 1)
        zero = jnp.zeros_like(q)
        return (jnp.where(lane < DA_QK_DIM, q, zero), jnp.where(lane >= DA_QK_DIM, q, zero))

    def scores_t(kblk, qmi):
        return lax.dot_general(kblk, qmi, dims, preferred_element_type=jnp.float32)

    def scores(slot, jb, qm):
        off = pl.multiple_of(jb * tkb, tkb)
        kblk = k_ref[0, pl.ds(off, tkb), :]
        for i in range(2):
            st = scores_t(kblk, qm[i])
            s_ref[slot, i] = st
            cm_ref[slot, i] = jnp.max(st, axis=0, keepdims=True)

    def online_update(i, st, cm, vt_blk):
        m_prev = m_ref[i]
        m_new = jnp.maximum(m_prev, cm)
        alpha = jnp.exp2(m_prev - m_new)
        p = jnp.exp2(st - m_new)
        acc_ref[i] = alpha * acc_ref[i] + jnp.dot(
            vt_blk, p.astype(jnp.bfloat16), preferred_element_type=jnp.float32)
        m_ref[i] = m_new

    def consume(slot, jb):
        off = pl.multiple_of(jb * tkb, tkb)
        vt_blk = vt_ref[:, pl.ds(off, tkb)]
        for i in range(2):
            online_update(i, s_ref[slot, i], cm_ref[slot, i], vt_blk)

    scores(0, 0, masked_q(0))

    def q_block(qi, carry):
        qm = masked_q(qi)
        m_ref[...] = jnp.full(m_ref.shape, -jnp.inf, jnp.float32)
        acc_ref[...] = jnp.zeros(acc_ref.shape, jnp.float32)

        for jb in range(nkv - 1):
            scores((jb + 1) % 2, jb + 1, qm)
            consume(jb % 2, jb)
        kctx = kc_ref[0]
        s_ctx = [scores_t(kctx, qm[i]) for i in range(2)]
        scores(0, 0, masked_q(jnp.minimum(qi + 1, nq - 1)))
        consume((nkv - 1) % 2, nkv - 1)
        for i in range(2):
            online_update(i, s_ctx[i], jnp.max(s_ctx[i], axis=0, keepdims=True), vct_ref[...])

        attn_t = (acc_ref[0, :dv] * (1.0 / acc_ref[0, dv:dv + 1])
                  - lam * (acc_ref[1, :dv] * (1.0 / acc_ref[1, dv:dv + 1])))
        attn = attn_t.T
        ms = jnp.mean(attn * attn, axis=-1, keepdims=True)
        y = attn * lax.rsqrt(ms + SUBLN_EPS) * subg_ref[...]
        y = y * (1.0 - lambda_init)
        rows = pl.ds(pl.multiple_of(qi * tq, tq), tq)
        o_ref[0, rows, :] = (y * sg_ref[0, rows, :].astype(jnp.float32)).astype(o_ref.dtype)
        return carry

    lax.fori_loop(0, nq, q_block, 0)


def _diff_attention(proj, ctxp, lam_p, subln_g, *, lambda_init, tq, tkb):
    b, seq, _ = proj.shape
    n_ctx = ctxp.shape[1]
    hb = DA_HEADS
    assert seq % (2 * tkb) == 0 and seq // tkb >= 2
    ones_rows = 16
    kernel = functools.partial(_diff_attn_kernel, tkb=tkb, lambda_init=lambda_init)
    return pl.pallas_call(
        kernel,
        grid=(b, DA_HEADS),
        in_specs=[
            pl.BlockSpec((1, seq, LANES), lambda bi, h: (bi, 0, COL_QA * hb + h)),
            pl.BlockSpec((1, seq, LANES), lambda bi, h: (bi, 0, COL_KA * hb + h)),
            pl.BlockSpec((1, seq, LANES), lambda bi, h: (bi, 0, COL_VA * hb + h)),
            pl.BlockSpec((1, n_ctx, LANES), lambda bi, h: (bi, 0, 0 * hb + h)),
            pl.BlockSpec((1, n_ctx, LANES), lambda bi, h: (bi, 0, 1 * hb + h)),
            pl.BlockSpec((1, seq, LANES), lambda bi, h: (bi, 0, COL_GA * hb + h)),
            pl.BlockSpec(lam_p.shape, lambda bi, h: (0, 0)),
            pl.BlockSpec((1, LANES), lambda bi, h: (0, 0)),
        ],
        out_specs=pl.BlockSpec((1, seq, LANES), lambda bi, h: (bi, 0, h)),
        out_shape=jax.ShapeDtypeStruct((b, seq, DA_HEADS * DA_V_DIM), jnp.bfloat16),
        scratch_shapes=[
            pltpu.VMEM((2, 2, tkb, tq), jnp.float32),
            pltpu.VMEM((2, 2, 1, tq), jnp.float32),
            pltpu.VMEM((2, 1, tq), jnp.float32),
            pltpu.VMEM((2, DA_V_DIM + ones_rows, tq), jnp.float32),
            pltpu.VMEM((DA_V_DIM + ones_rows, seq), jnp.bfloat16),
            pltpu.VMEM((DA_V_DIM + ones_rows, n_ctx), jnp.bfloat16),
        ],
        compiler_params=pltpu.CompilerParams(
            dimension_semantics=("parallel", "parallel"), vmem_limit_bytes=VMEM_LIMIT_BYTES),
        name="diff_attention",
    )(proj, proj, proj, ctxp, ctxp, proj, lam_p, subln_g.reshape(1, LANES))


NA_BLOCK_ROWS = NA_WIN_ROWS // 2
NA_BAND_ROWS = 3 * NA_BLOCK_ROWS


def _na_bias_table(rpb_l, rows):
    r_blk, r_band = NA_BLOCK_ROWS, NA_BAND_ROWS
    n_blocks = rows // r_blk
    row_off = np.zeros((3, r_blk, r_band), np.int32)
    row_ok = np.zeros((3, r_blk, r_band), bool)
    for variant, rb in enumerate((0, 1, n_blocks - 1)):
        band0 = min(max(rb - 1, 0), n_blocks - 3) * r_blk
        for qr in range(r_blk):
            r = rb * r_blk + qr
            r0 = min(max(r - NA_WIN_ROWS // 2, 0), rows - NA_WIN_ROWS)
            for kr in range(r_band):
                ka = band0 + kr
                row_ok[variant, qr, kr] = r0 <= ka < r0 + NA_WIN_ROWS
                row_off[variant, qr, kr] = np.clip(ka - r + NA_WIN_ROWS - 1, 0, 2 * NA_WIN_ROWS - 2)
    qc = np.arange(GRID_W)
    c0 = np.clip(qc - NA_WIN_COLS // 2, 0, GRID_W - NA_WIN_COLS)
    kc = np.arange(GRID_W)
    col_ok = (kc[None, :] >= c0[:, None]) & (kc[None, :] < c0[:, None] + NA_WIN_COLS)
    heads, n_ro, _ = rpb_l.shape
    pad = GRID_W - NA_WIN_COLS
    rp = jnp.pad(rpb_l.astype(jnp.float32) * LOG2E, ((0, 0), (0, 0), (pad, pad)))
    tiles = jnp.stack([rp[:, :, GRID_W - 1 - q:2 * GRID_W - 1 - q] for q in range(GRID_W)],
                      axis=2)
    tiles = jnp.where(jnp.asarray(col_ok)[None, None], tiles, MASK_VALUE)
    masked = jnp.full((heads, GRID_W, GRID_W), MASK_VALUE, jnp.float32)
    variants = []
    for variant in range(3):
        block_rows = []
        for qr in range(r_blk):
            block_rows.append(jnp.concatenate(
                [tiles[:, row_off[variant, qr, kr]] if row_ok[variant, qr, kr] else masked
                 for kr in range(r_band)], axis=-1))
        variants.append(jnp.concatenate(block_rows, axis=-2))
    return jnp.stack(variants, axis=1)


def _na_kernel(q_ref, k_ref, v_ref, kc_ref, vc_ref, sg_ref, bias_ref, o_ref,
               s_ref, v1_ref, vc1_ref):
    seq = q_ref.shape[1]
    dv = v_ref.shape[2]
    n_ctx = kc_ref.shape[1]
    tq = NA_BLOCK_ROWS * GRID_W
    band = NA_BAND_ROWS * GRID_W
    n_blocks = seq // tq
    kctx = kc_ref[0]
    dims = (((1,), (1,)), ((), ()))

    v1_ref[:, :dv] = v_ref[0]
    v1_ref[:, dv:] = jnp.ones((seq, dv), v1_ref.dtype)
    vc1_ref[:, :dv] = vc_ref[0]
    vc1_ref[:, dv:] = jnp.ones((n_ctx, dv), vc1_ref.dtype)

    def band_start(rb):
        return pl.multiple_of(jnp.clip(rb - 1, 0, n_blocks - 3) * tq, tq)

    def scores(slot, rb):
        variant = jnp.where(rb == 0, 0, jnp.where(rb == n_blocks - 1, 2, 1))
        q = q_ref[0, pl.ds(pl.multiple_of(rb * tq, tq), tq), :]
        s_ref[slot, :, :band] = lax.dot_general(
            q, k_ref[0, pl.ds(band_start(rb), band), :], dims,
            preferred_element_type=jnp.float32) + bias_ref[0, variant]
        s_ref[slot, :, band:] = lax.dot_general(q, kctx, dims,
                                                preferred_element_type=jnp.float32)

    def consume(slot, rb):
        s = s_ref[slot]
        p = jnp.exp2(s - jnp.max(s, axis=1, keepdims=True)).astype(jnp.bfloat16)
        o = (jnp.dot(p[:, :band], v1_ref[pl.ds(band_start(rb), band), :],
                     preferred_element_type=jnp.float32)
             + jnp.dot(p[:, band:], vc1_ref[...], preferred_element_type=jnp.float32))
        o = o[:, :dv] * (1.0 / o[:, dv:])
        rows = pl.ds(pl.multiple_of(rb * tq, tq), tq)
        o_ref[0, rows, :] = (o * sg_ref[0, rows, :].astype(jnp.float32)).astype(o_ref.dtype)

    scores(0, 0)

    def body(rr, carry):
        rb = 2 * rr
        scores(1, rb + 1)
        consume(0, rb)
        scores(0, jnp.minimum(rb + 2, n_blocks - 1))
        consume(1, rb + 1)
        return carry

    lax.fori_loop(0, n_blocks // 2, body, 0, unroll=2)


def _neighbourhood_attention(proj, ctxp, bias):
    b, seq, _ = proj.shape
    n_ctx = ctxp.shape[1]
    hb = NA_HEADS
    tq = NA_BLOCK_ROWS * GRID_W
    band = NA_BAND_ROWS * GRID_W
    return pl.pallas_call(
        _na_kernel,
        grid=(b, NA_HEADS),
        in_specs=[
            pl.BlockSpec((1, seq, LANES), lambda bi, h: (bi, 0, COL_QB * hb + h)),
            pl.BlockSpec((1, seq, LANES), lambda bi, h: (bi, 0, COL_KB * hb + h)),
            pl.BlockSpec((1, seq, LANES), lambda bi, h: (bi, 0, COL_VB * hb + h)),
            pl.BlockSpec((1, n_ctx, LANES), lambda bi, h: (bi, 0, 2 * hb + h)),
            pl.BlockSpec((1, n_ctx, LANES), lambda bi, h: (bi, 0, 3 * hb + h)),
            pl.BlockSpec((1, seq, LANES), lambda bi, h: (bi, 0, COL_GB * hb + h)),
            pl.BlockSpec((1, 3, tq, band), lambda bi, h: (h, 0, 0, 0)),
        ],
        out_specs=pl.BlockSpec((1, seq, LANES), lambda bi, h: (bi, 0, h)),
        out_shape=jax.ShapeDtypeStruct((b, seq, NA_HEADS * NA_HEAD_DIM), jnp.bfloat16),
        scratch_shapes=[
            pltpu.VMEM((2, tq, band + n_ctx), jnp.float32),
            pltpu.VMEM((seq, 2 * NA_HEAD_DIM), jnp.bfloat16),
            pltpu.VMEM((n_ctx, 2 * NA_HEAD_DIM), jnp.bfloat16),
        ],
        compiler_params=pltpu.CompilerParams(
            dimension_semantics=("parallel", "parallel"), vmem_limit_bytes=VMEM_LIMIT_BYTES),
        name="neighbourhood_attention",
    )(proj, proj, proj, ctxp, ctxp, proj, bias)


def _out_kernel(oa_ref, ob_ref, wa_ref, wb_ref, x_ref, gate_ref, fg_ref, o_ref, *, final_norm):
    half = oa_ref.shape[0] // 2
    for r in range(2):
        rows = slice(r * half, (r + 1) * half)
        mixed = (jnp.dot(oa_ref[rows, :], wa_ref[...], preferred_element_type=jnp.float32)
                 + jnp.dot(ob_ref[rows, :], wb_ref[...], preferred_element_type=jnp.float32))
        h = x_ref[rows, :] + gate_ref[0] * mixed
        if final_norm:
            ms = jnp.mean(h * h, axis=-1, keepdims=True)
            h = h * lax.rsqrt(ms + NORM_EPS) * fg_ref[...]
        o_ref[rows, :] = h


def _out_projection(oa2, ob2, w_out_bf16, x2, gate, final_g, *, tm, rows_per_mod, final_norm):
    m, d = x2.shape
    wa_rows = oa2.shape[1]
    wb_rows = ob2.shape[1]
    assert wa_rows == wb_rows
    blocks_per_mod = rows_per_mod // tm
    kernel = functools.partial(_out_kernel, final_norm=final_norm)
    return pl.pallas_call(
        kernel,
        grid=(m // tm,),
        in_specs=[
            pl.BlockSpec((tm, wa_rows), lambda i: (i, 0)),
            pl.BlockSpec((tm, wb_rows), lambda i: (i, 0)),
            pl.BlockSpec((wa_rows, d), lambda i: (0, 0)),
            pl.BlockSpec((wb_rows, d), lambda i: (1, 0)),
            pl.BlockSpec((tm, d), lambda i: (i, 0)),
            pl.BlockSpec((1, 1, d), lambda i: (i // blocks_per_mod, 0, 0)),
            pl.BlockSpec((1, d), lambda i: (0, 0)),
        ],
        out_specs=pl.BlockSpec((tm, d), lambda i: (i, 0)),
        out_shape=jax.ShapeDtypeStruct((m, d), jnp.float32),
        compiler_params=pltpu.CompilerParams(
            dimension_semantics=("parallel",), vmem_limit_bytes=VMEM_LIMIT_BYTES),
        name="out_projection",
    )(oa2, ob2, w_out_bf16, w_out_bf16, x2, gate, final_g.reshape(1, d))


def kernel(x, c, ctx, c_ctx, norm_g, w_mod, b_mod, w_in, w_out, lam_q1, lam_k1, lam_q2, lam_k2,
           subln_g, rpb, final_g):
    b, seq, d = x.shape
    n_ctx = ctx.shape[1]
    depth = w_in.shape[0]
    rows = seq // GRID_W
    group = DA_HEADS * DA_V_DIM
    assert w_in.shape[2] == 8 * group and NA_HEADS * NA_HEAD_DIM == group
    assert DA_V_DIM == LANES and NA_HEAD_DIM == LANES

    rope_tabs = _rope_tables(seq)
    q_scales = (DA_QK_DIM ** -0.5 * LOG2E, NA_HEAD_DIM ** -0.5 * LOG2E)
    x_kinds = ("rope_q", "rope_k", "plain", "silu", "scale_qb", "plain", "plain", "silu")
    ctx_cols = (COL_KA, COL_VA, COL_KB, COL_VB)

    mod_rows = 8
    cc = jnp.zeros((mod_rows, d), jnp.float32).at[:b].set(c).at[b].set(c_ctx)
    ctx2 = ctx.reshape(b * n_ctx, d)

    h2 = x.reshape(b * seq, d)
    for layer in range(depth):
        lambda_init = 0.8 - 0.6 * math.exp(-0.3 * layer)
        mod = _modulation(cc, w_mod[layer], b_mod[layer])
        shift = mod[:b, :d].reshape(b, 1, d)
        scale = mod[:b, d:2 * d].reshape(b, 1, d)
        gate = mod[:b, 2 * d:].reshape(b, 1, d)
        shift_c = mod[b:b + 1, :d].reshape(1, 1, d)
        scale_c = mod[b:b + 1, d:2 * d].reshape(1, 1, d)

        wi = w_in[layer].astype(jnp.bfloat16)
        proj = _in_projection(
            h2, norm_g[layer], scale, shift, wi, tm=1024, tn=group, rows_per_mod=seq,
            col_blocks=tuple(range(8)), kinds=x_kinds, q_scales=q_scales, rope_tabs=rope_tabs,
            name="in_projection").reshape(b, seq, 8 * group)
        ctxp = _in_projection(
            ctx2, norm_g[layer], scale_c, shift_c, wi, tm=b * n_ctx, tn=group,
            rows_per_mod=b * n_ctx, col_blocks=ctx_cols, kinds=("plain",) * 4, q_scales=q_scales,
            rope_tabs=None, name="ctx_projection").reshape(b, n_ctx, 4 * group)

        lam_p = jnp.stack([lam_q1[layer], lam_k1[layer], lam_q2[layer], lam_k2[layer]]
                          ).astype(jnp.float32)
        oa = _diff_attention(proj, ctxp, lam_p, subln_g[layer], lambda_init=lambda_init,
                             tq=512, tkb=512)
        ob = _neighbourhood_attention(proj, ctxp, _na_bias_table(rpb[layer], rows))

        h2 = _out_projection(
            oa.reshape(b * seq, group), ob.reshape(b * seq, group),
            w_out[layer].astype(jnp.bfloat16), h2, gate, final_g, tm=512, rows_per_mod=seq,
            final_norm=(layer == depth - 1))
    return h2.reshape(b, seq, d)
```

```python
import functools
import math

import numpy as np
import jax
import jax.numpy as jnp
from jax import lax
from jax.experimental import pallas as pl
from jax.experimental.pallas import tpu as pltpu

GRID_W = 64
DA_HEADS = 8
DA_QK_DIM = 64
DA_V_DIM = 2 * DA_QK_DIM
NA_HEADS = 8
NA_HEAD_DIM = 128
NA_WIN_ROWS = 8
NA_WIN_COLS = 16
ROPE_BASE = 10000.0
NORM_EPS = 1e-6
SUBLN_EPS = 1e-5

LANES = 128
VMEM_LIMIT_BYTES = 56 * 1024 * 1024
LOG2E = math.log2(math.e)
MASK_VALUE = -1e30

COL_QA, COL_KA, COL_VA, COL_GA, COL_QB, COL_KB, COL_VB, COL_GB = range(8)


def _silu(x):
    return x * (1.0 / (1.0 + jnp.exp(-x)))


def _mod_kernel(c_ref, w_ref, b_ref, o_ref):
    a = _silu(c_ref[...]).astype(jnp.bfloat16)
    w = w_ref[...].astype(jnp.bfloat16)
    o_ref[...] = jnp.dot(a, w, preferred_element_type=jnp.float32) + b_ref[...]


def _modulation(cc, w_mod, b_mod):
    rows, d = cc.shape
    n = w_mod.shape[1]
    tn = 768
    return pl.pallas_call(
        _mod_kernel,
        grid=(n // tn,),
        in_specs=[
            pl.BlockSpec((rows, d), lambda j: (0, 0)),
            pl.BlockSpec((d, tn), lambda j: (0, j)),
            pl.BlockSpec((1, tn), lambda j: (0, j)),
        ],
        out_specs=pl.BlockSpec((rows, tn), lambda j: (0, j)),
        out_shape=jax.ShapeDtypeStruct((rows, n), jnp.float32),
        compiler_params=pltpu.CompilerParams(
            dimension_semantics=("parallel",), vmem_limit_bytes=VMEM_LIMIT_BYTES),
        name="modulation",
    )(cc, w_mod, b_mod.reshape(1, n))


def _rope_tables(seq):
    t = np.arange(seq)
    row_pos = (t // GRID_W).astype(np.float32)
    col_pos = (t % GRID_W).astype(np.float32)
    half = DA_QK_DIM // 4
    inv_freq = (np.float32(ROPE_BASE) ** (-np.arange(half, dtype=np.float32) / np.float32(half))
                ).astype(np.float32)
    lane = np.arange(LANES)
    d = lane % DA_QK_DIM
    use_col = (d // (DA_QK_DIM // 2)) == 1
    f = d % half
    first_half = (d % (2 * half)) < half
    pos = np.where(use_col[None, :], col_pos[:, None], row_pos[:, None])
    ang = (pos * inv_freq[f][None, :]).astype(np.float32)
    cos = np.cos(ang.astype(np.float64)).astype(np.float32)
    sin = np.sin(ang.astype(np.float64)).astype(np.float32)
    sin_signed = np.where(first_half[None, :], -sin, sin)
    return jnp.asarray(cos), jnp.asarray(sin_signed)


def _proj_kernel(*refs, kinds, q_scales, with_rope):
    if with_rope:
        x_ref, g_ref, scale_ref, shift_ref, w_ref, cos_ref, sin_ref, o_ref, hx_ref = refs
    else:
        x_ref, g_ref, scale_ref, shift_ref, w_ref, o_ref, hx_ref = refs
    j = pl.program_id(1)

    def normed_input():
        x = x_ref[...]
        ms = jnp.mean(x * x, axis=-1, keepdims=True)
        y = x * lax.rsqrt(ms + NORM_EPS) * g_ref[...]
        hx = (y * (1.0 + scale_ref[0]) + shift_ref[0]).astype(jnp.bfloat16)
        hx_ref[...] = hx
        return hx

    def rope_store(acc, s):
        a = cos_ref[...]
        b = sin_ref[...]
        if s != 1.0:
            a = a * s
            b = b * s
        lane = lax.broadcasted_iota(jnp.int32, a.shape, 1)
        half = DA_QK_DIM // 4
        first_half = (lane % (2 * half)) < half
        for hh in range(acc.shape[1] // LANES):
            xs = acc[:, hh * LANES:(hh + 1) * LANES]
            fwd = pltpu.roll(xs, LANES - half, 1)
            bwd = pltpu.roll(xs, half, 1)
            y = xs * a + jnp.where(first_half, fwd, bwd) * b
            o_ref[:, hh * LANES:(hh + 1) * LANES] = y.astype(o_ref.dtype)

    def epilogue(kind, acc):
        if kind == "rope_q":
            rope_store(acc, q_scales[0])
        elif kind == "rope_k":
            rope_store(acc, 1.0)
        elif kind == "silu":
            o_ref[...] = _silu(acc).astype(o_ref.dtype)
        elif kind == "scale_qb":
            o_ref[...] = (acc * q_scales[1]).astype(o_ref.dtype)
        else:
            o_ref[...] = acc.astype(o_ref.dtype)

    @pl.when(j == 0)
    def _():
        hx = normed_input()
        epilogue(kinds[0], jnp.dot(hx, w_ref[...], preferred_element_type=jnp.float32))

    for kind in sorted(set(kinds[1:])):
        cols = [jj for jj in range(1, len(kinds)) if kinds[jj] == kind]
        cond = functools.reduce(jnp.logical_or, [j == jj for jj in cols])

        @pl.when(cond)
        def _(kind=kind):
            epilogue(kind, jnp.dot(hx_ref[...], w_ref[...], preferred_element_type=jnp.float32))


def _in_projection(x2, norm_g, scale, shift, w_bf16, *, tm, tn, rows_per_mod, col_blocks,
                   kinds, q_scales, rope_tabs, name):
    m, d = x2.shape
    nj = len(col_blocks)
    with_rope = rope_tabs is not None
    blocks_per_mod = rows_per_mod // tm
    start, skip = col_blocks[0], 0
    if nj > 2:
        skip = col_blocks[2] - col_blocks[1] - 1
    assert all(col_blocks[jj] == start + jj + skip * (jj // 2) for jj in range(nj))

    def w_map(i, j):
        return (0, start + j + skip * (j // 2))

    in_specs = [
        pl.BlockSpec((tm, d), lambda i, j: (i, 0)),
        pl.BlockSpec((1, d), lambda i, j: (0, 0)),
        pl.BlockSpec((1, 1, d), lambda i, j: (i // blocks_per_mod, 0, 0)),
        pl.BlockSpec((1, 1, d), lambda i, j: (i // blocks_per_mod, 0, 0)),
        pl.BlockSpec((d, tn), w_map),
    ]
    args = [x2, norm_g.reshape(1, d), scale, shift, w_bf16]
    if with_rope:
        seq_blocks = rope_tabs[0].shape[0] // tm
        in_specs += [pl.BlockSpec((tm, LANES), lambda i, j: (i % seq_blocks, 0))] * 2
        args += list(rope_tabs)
    kernel = functools.partial(_proj_kernel, kinds=tuple(kinds), q_scales=q_scales,
                               with_rope=with_rope)
    return pl.pallas_call(
        kernel,
        grid=(m // tm, nj),
        in_specs=in_specs,
        out_specs=pl.BlockSpec((tm, tn), lambda i, j: (i, j)),
        out_shape=jax.ShapeDtypeStruct((m, nj * tn), jnp.bfloat16),
        scratch_shapes=[pltpu.VMEM((tm, d), jnp.bfloat16)],
        compiler_params=pltpu.CompilerParams(
            dimension_semantics=("parallel", "arbitrary"), vmem_limit_bytes=VMEM_LIMIT_BYTES),
        name=name,
    )(*args)


def _diff_attn_kernel(q_ref, k_ref, v_ref, kc_ref, vc_ref, sg_ref, lam_ref, subg_ref, o_ref,
                      s_ref, cm_ref, m_ref, acc_ref, vt_ref, vct_ref, *, tkb, lambda_init):
    seq = k_ref.shape[1]
    tq = s_ref.shape[3]
    nkv = seq // tkb
    nq = seq // tq
    dims = (((1,), (1,)), ((), ()))

    dv = v_ref.shape[2]
    for c in range(nkv):
        vt_ref[:dv, c * tkb:(c + 1) * tkb] = v_ref[0, c * tkb:(c + 1) * tkb, :].T
    vt_ref[dv:, :] = jnp.ones((vt_ref.shape[0] - dv, seq), vt_ref.dtype)
    vct_ref[:dv, :] = vc_ref[0].T
    vct_ref[dv:, :] = jnp.ones((vct_ref.shape[0] - dv, vct_ref.shape[1]), vct_ref.dtype)

    lam_p = lam_ref[...]
    lam = (jnp.exp(jnp.sum(lam_p[0:1] * lam_p[1:2], axis=1, keepdims=True))
           - jnp.exp(jnp.sum(lam_p[2:3] * lam_p[3:4], axis=1, keepdims=True)) + lambda_init)

    def masked_q(qi):
        q = q_ref[0, pl.ds(pl.multiple_of(qi * tq, tq), tq), :]
        lane = lax.broadcasted_iota(jnp.int32, q.shape, 1)
        zero = jnp.zeros_like(q)
        return (jnp.where(lane < DA_QK_DIM, q, zero), jnp.where(lane >= DA_QK_DIM, q, zero))

    def scores_t(kblk, qmi):
        return lax.dot_general(kblk, qmi, dims, preferred_element_type=jnp.float32)

    def scores(slot, jb, qm):
        off = pl.multiple_of(jb * tkb, tkb)
        kblk = k_ref[0, pl.ds(off, tkb), :]
        for i in range(2):
            st = scores_t(kblk, qm[i])
            s_ref[slot, i] = st
            cm_ref[slot, i] = jnp.max(st, axis=0, keepdims=True)

    def online_update(i, st, cm, vt_blk):
        m_prev = m_ref[i]
        m_new = jnp.maximum(m_prev, cm)
        alpha = jnp.exp2(m_prev - m_new)
        p = jnp.exp2(st - m_new)
        acc_ref[i] = alpha * acc_ref[i] + jnp.dot(
            vt_blk, p.astype(jnp.bfloat16), preferred_element_type=jnp.float32)
        m_ref[i] = m_new

    def consume(slot, jb):
        off = pl.multiple_of(jb * tkb, tkb)
        vt_blk = vt_ref[:, pl.ds(off, tkb)]
        for i in range(2):
            online_update(i, s_ref[slot, i], cm_ref[slot, i], vt_blk)

    scores(0, 0, masked_q(0))

    def q_block(qi, carry):
        qm = masked_q(qi)
        m_ref[...] = jnp.full(m_ref.shape, -jnp.inf, jnp.float32)
        acc_ref[...] = jnp.zeros(acc_ref.shape, jnp.float32)

        for jb in range(nkv - 1):
            scores((jb + 1) % 2, jb + 1, qm)
            consume(jb % 2, jb)
        kctx = kc_ref[0]
        s_ctx = [scores_t(kctx, qm[i]) for i in range(2)]
        scores(0, 0, masked_q(jnp.minimum(qi + 1, nq - 1)))
        consume((nkv - 1) % 2, nkv - 1)
        for i in range(2):
            online_update(i, s_ctx[i], jnp.max(s_ctx[i], axis=0, keepdims=True), vct_ref[...])

        attn_t = (acc_ref[0, :dv] * (1.0 / acc_ref[0, dv:dv + 1])
                  - lam * (acc_ref[1, :dv] * (1.0 / acc_ref[1, dv:dv + 1])))
        attn = attn_t.T
        ms = jnp.mean(attn * attn, axis=-1, keepdims=True)
        y = attn * lax.rsqrt(ms + SUBLN_EPS) * subg_ref[...]
        y = y * (1.0 - lambda_init)
        rows = pl.ds(pl.multiple_of(qi * tq, tq), tq)
        o_ref[0, rows, :] = (y * sg_ref[0, rows, :].astype(jnp.float32)).astype(o_ref.dtype)
        return carry

    lax.fori_loop(0, nq, q_block, 0)


def _diff_attention(proj, ctxp, lam_p, subln_g, *, lambda_init, tq, tkb):
    b, seq, _ = proj.shape
    n_ctx = ctxp.shape[1]
    hb = DA_HEADS
    assert seq % (2 * tkb) == 0 and seq // tkb >= 2
    ones_rows = 16
    kernel = functools.partial(_diff_attn_kernel, tkb=tkb, lambda_init=lambda_init)
    return pl.pallas_call(
        kernel,
        grid=(b, DA_HEADS),
        in_specs=[
            pl.BlockSpec((1, seq, LANES), lambda bi, h: (bi, 0, COL_QA * hb + h)),
            pl.BlockSpec((1, seq, LANES), lambda bi, h: (bi, 0, COL_KA * hb + h)),
            pl.BlockSpec((1, seq, LANES), lambda bi, h: (bi, 0, COL_VA * hb + h)),
            pl.BlockSpec((1, n_ctx, LANES), lambda bi, h: (bi, 0, 0 * hb + h)),
            pl.BlockSpec((1, n_ctx, LANES), lambda bi, h: (bi, 0, 1 * hb + h)),
            pl.BlockSpec((1, seq, LANES), lambda bi, h: (bi, 0, COL_GA * hb + h)),
            pl.BlockSpec(lam_p.shape, lambda bi, h: (0, 0)),
            pl.BlockSpec((1, LANES), lambda bi, h: (0, 0)),
        ],
        out_specs=pl.BlockSpec((1, seq, LANES), lambda bi, h: (bi, 0, h)),
        out_shape=jax.ShapeDtypeStruct((b, seq, DA_HEADS * DA_V_DIM), jnp.bfloat16),
        scratch_shapes=[
            pltpu.VMEM((2, 2, tkb, tq), jnp.float32),
            pltpu.VMEM((2, 2, 1, tq), jnp.float32),
            pltpu.VMEM((2, 1, tq), jnp.float32),
            pltpu.VMEM((2, DA_V_DIM + ones_rows, tq), jnp.float32),
            pltpu.VMEM((DA_V_DIM + ones_rows, seq), jnp.bfloat16),
            pltpu.VMEM((DA_V_DIM + ones_rows, n_ctx), jnp.bfloat16),
        ],
        compiler_params=pltpu.CompilerParams(
            dimension_semantics=("parallel", "parallel"), vmem_limit_bytes=VMEM_LIMIT_BYTES),
        name="diff_attention",
    )(proj, proj, proj, ctxp, ctxp, proj, lam_p, subln_g.reshape(1, LANES))


NA_BLOCK_ROWS = NA_WIN_ROWS // 2
NA_BAND_ROWS = 3 * NA_BLOCK_ROWS


def _na_bias_table(rpb_l, rows):
    r_blk, r_band = NA_BLOCK_ROWS, NA_BAND_ROWS
    n_blocks = rows // r_blk
    row_off = np.zeros((3, r_blk, r_band), np.int32)
    row_ok = np.zeros((3, r_blk, r_band), bool)
    for variant, rb in enumerate((0, 1, n_blocks - 1)):
        band0 = min(max(rb - 1, 0), n_blocks - 3) * r_blk
        for qr in range(r_blk):
            r = rb * r_blk + qr
            r0 = min(max(r - NA_WIN_ROWS // 2, 0), rows - NA_WIN_ROWS)
            for kr in range(r_band):
                ka = band0 + kr
                row_ok[variant, qr, kr] = r0 <= ka < r0 + NA_WIN_ROWS
                row_off[variant, qr, kr] = np.clip(ka - r + NA_WIN_ROWS - 1, 0, 2 * NA_WIN_ROWS - 2)
    qc = np.arange(GRID_W)
    c0 = np.clip(qc - NA_WIN_COLS // 2, 0, GRID_W - NA_WIN_COLS)
    kc = np.arange(GRID_W)
    col_ok = (kc[None, :] >= c0[:, None]) & (kc[None, :] < c0[:, None] + NA_WIN_COLS)
    heads, n_ro, _ = rpb_l.shape
    pad = GRID_W - NA_WIN_COLS
    rp = jnp.pad(rpb_l.astype(jnp.float32) * LOG2E, ((0, 0), (0, 0), (pad, pad)))
    tiles = jnp.stack([rp[:, :, GRID_W - 1 - q:2 * GRID_W - 1 - q] for q in range(GRID_W)],
                      axis=2)
    tiles = jnp.where(jnp.asarray(col_ok)[None, None], tiles, MASK_VALUE)
    masked = jnp.full((heads, GRID_W, GRID_W), MASK_VALUE, jnp.float32)
    variants = []
    for variant in range(3):
        block_rows = []
        for qr in range(r_blk):
            block_rows.append(jnp.concatenate(
                [tiles[:, row_off[variant, qr, kr]] if row_ok[variant, qr, kr] else masked
                 for kr in range(r_band)], axis=-1))
        variants.append(jnp.concatenate(block_rows, axis=-2))
    return jnp.stack(variants, axis=1)


def _na_kernel(q_ref, k_ref, v_ref, kc_ref, vc_ref, sg_ref, bias_ref, o_ref,
               s_ref, v1_ref, vc1_ref):
    seq = q_ref.shape[1]
    dv = v_ref.shape[2]
    n_ctx = kc_ref.shape[1]
    tq = NA_BLOCK_ROWS * GRID_W
    band = NA_BAND_ROWS * GRID_W
    n_blocks = seq // tq
    kctx = kc_ref[0]
    dims = (((1,), (1,)), ((), ()))

    v1_ref[:, :dv] = v_ref[0]
    v1_ref[:, dv:] = jnp.ones((seq, dv), v1_ref.dtype)
    vc1_ref[:, :dv] = vc_ref[0]
    vc1_ref[:, dv:] = jnp.ones((n_ctx, dv), vc1_ref.dtype)

    def band_start(rb):
        return pl.multiple_of(jnp.clip(rb - 1, 0, n_blocks - 3) * tq, tq)

    def scores(slot, rb):
        variant = jnp.where(rb == 0, 0, jnp.where(rb == n_blocks - 1, 2, 1))
        q = q_ref[0, pl.ds(pl.multiple_of(rb * tq, tq), tq), :]
        s_ref[slot, :, :band] = lax.dot_general(
            q, k_ref[0, pl.ds(band_start(rb), band), :], dims,
            preferred_element_type=jnp.float32) + bias_ref[0, variant]
        s_ref[slot, :, band:] = lax.dot_general(q, kctx, dims,
                                                preferred_element_type=jnp.float32)

    def consume(slot, rb):
        s = s_ref[slot]
        p = jnp.exp2(s - jnp.max(s, axis=1, keepdims=True)).astype(jnp.bfloat16)
        o = (jnp.dot(p[:, :band], v1_ref[pl.ds(band_start(rb), band), :],
                     preferred_element_type=jnp.float32)
             + jnp.dot(p[:, band:], vc1_ref[...], preferred_element_type=jnp.float32))
        o = o[:, :dv] * (1.0 / o[:, dv:])
        rows = pl.ds(pl.multiple_of(rb * tq, tq), tq)
        o_ref[0, rows, :] = (o * sg_ref[0, rows, :].astype(jnp.float32)).astype(o_ref.dtype)

    scores(0, 0)

    def body(rr, carry):
        rb = 2 * rr
        scores(1, rb + 1)
        consume(0, rb)
        scores(0, jnp.minimum(rb + 2, n_blocks - 1))
        consume(1, rb + 1)
        return carry

    lax.fori_loop(0, n_blocks // 2, body, 0, unroll=8)


def _neighbourhood_attention(proj, ctxp, bias):
    b, seq, _ = proj.shape
    n_ctx = ctxp.shape[1]
    hb = NA_HEADS
    tq = NA_BLOCK_ROWS * GRID_W
    band = NA_BAND_ROWS * GRID_W
    return pl.pallas_call(
        _na_kernel,
        grid=(b, NA_HEADS),
        in_specs=[
            pl.BlockSpec((1, seq, LANES), lambda bi, h: (bi, 0, COL_QB * hb + h)),
            pl.BlockSpec((1, seq, LANES), lambda bi, h: (bi, 0, COL_KB * hb + h)),
            pl.BlockSpec((1, seq, LANES), lambda bi, h: (bi, 0, COL_VB * hb + h)),
            pl.BlockSpec((1, n_ctx, LANES), lambda bi, h: (bi, 0, 2 * hb + h)),
            pl.BlockSpec((1, n_ctx, LANES), lambda bi, h: (bi, 0, 3 * hb + h)),
            pl.BlockSpec((1, seq, LANES), lambda bi, h: (bi, 0, COL_GB * hb + h)),
            pl.BlockSpec((1, 3, tq, band), lambda bi, h: (h, 0, 0, 0)),
        ],
        out_specs=pl.BlockSpec((1, seq, LANES), lambda bi, h: (bi, 0, h)),
        out_shape=jax.ShapeDtypeStruct((b, seq, NA_HEADS * NA_HEAD_DIM), jnp.bfloat16),
        scratch_shapes=[
            pltpu.VMEM((2, tq, band + n_ctx), jnp.float32),
            pltpu.VMEM((seq, 2 * NA_HEAD_DIM), jnp.bfloat16),
            pltpu.VMEM((n_ctx, 2 * NA_HEAD_DIM), jnp.bfloat16),
        ],
        compiler_params=pltpu.CompilerParams(
            dimension_semantics=("parallel", "parallel"), vmem_limit_bytes=VMEM_LIMIT_BYTES),
        name="neighbourhood_attention",
    )(proj, proj, proj, ctxp, ctxp, proj, bias)


def _out_kernel(oa_ref, ob_ref, wa_ref, wb_ref, x_ref, gate_ref, fg_ref, o_ref, *, final_norm):
    half = oa_ref.shape[0] // 2
    for r in range(2):
        rows = slice(r * half, (r + 1) * half)
        mixed = (jnp.dot(oa_ref[rows, :], wa_ref[...], preferred_element_type=jnp.float32)
                 + jnp.dot(ob_ref[rows, :], wb_ref[...], preferred_element_type=jnp.float32))
        h = x_ref[rows, :] + gate_ref[0] * mixed
        if final_norm:
            ms = jnp.mean(h * h, axis=-1, keepdims=True)
            h = h * lax.rsqrt(ms + NORM_EPS) * fg_ref[...]
        o_ref[rows, :] = h


def _out_projection(oa2, ob2, w_out_bf16, x2, gate, final_g, *, tm, rows_per_mod, final_norm):
    m, d = x2.shape
    wa_rows = oa2.shape[1]
    wb_rows = ob2.shape[1]
    assert wa_rows == wb_rows
    blocks_per_mod = rows_per_mod // tm
    kernel = functools.partial(_out_kernel, final_norm=final_norm)
    return pl.pallas_call(
        kernel,
        grid=(m // tm,),
        in_specs=[
            pl.BlockSpec((tm, wa_rows), lambda i: (i, 0)),
            pl.BlockSpec((tm, wb_rows), lambda i: (i, 0)),
            pl.BlockSpec((wa_rows, d), lambda i: (0, 0)),
            pl.BlockSpec((wb_rows, d), lambda i: (1, 0)),
            pl.BlockSpec((tm, d), lambda i: (i, 0)),
            pl.BlockSpec((1, 1, d), lambda i: (i // blocks_per_mod, 0, 0)),
            pl.BlockSpec((1, d), lambda i: (0, 0)),
        ],
        out_specs=pl.BlockSpec((tm, d), lambda i: (i, 0)),
        out_shape=jax.ShapeDtypeStruct((m, d), jnp.float32),
        compiler_params=pltpu.CompilerParams(
            dimension_semantics=("parallel",), vmem_limit_bytes=VMEM_LIMIT_BYTES),
        name="out_projection",
    )(oa2, ob2, w_out_bf16, w_out_bf16, x2, gate, final_g.reshape(1, d))


def kernel(x, c, ctx, c_ctx, norm_g, w_mod, b_mod, w_in, w_out, lam_q1, lam_k1, lam_q2, lam_k2,
           subln_g, rpb, final_g):
    b, seq, d = x.shape
    n_ctx = ctx.shape[1]
    depth = w_in.shape[0]
    rows = seq // GRID_W
    group = DA_HEADS * DA_V_DIM
    assert w_in.shape[2] == 8 * group and NA_HEADS * NA_HEAD_DIM == group
    assert DA_V_DIM == LANES and NA_HEAD_DIM == LANES

    rope_tabs = _rope_tables(seq)
    q_scales = (DA_QK_DIM ** -0.5 * LOG2E, NA_HEAD_DIM ** -0.5 * LOG2E)
    x_kinds = ("rope_q", "rope_k", "plain", "silu", "scale_qb", "plain", "plain", "silu")
    ctx_cols = (COL_KA, COL_VA, COL_KB, COL_VB)

    mod_rows = 8
    cc = jnp.zeros((mod_rows, d), jnp.float32).at[:b].set(c).at[b].set(c_ctx)
    ctx2 = ctx.reshape(b * n_ctx, d)

    h2 = x.reshape(b * seq, d)
    for layer in range(depth):
        lambda_init = 0.8 - 0.6 * math.exp(-0.3 * layer)
        mod = _modulation(cc, w_mod[layer], b_mod[layer])
        shift = mod[:b, :d].reshape(b, 1, d)
        scale = mod[:b, d:2 * d].reshape(b, 1, d)
        gate = mod[:b, 2 * d:].reshape(b, 1, d)
        shift_c = mod[b:b + 1, :d].reshape(1, 1, d)
        scale_c = mod[b:b + 1, d:2 * d].reshape(1, 1, d)

        wi = w_in[layer].astype(jnp.bfloat16)
        proj = _in_projection(
            h2, norm_g[layer], scale, shift, wi, tm=1024, tn=group, rows_per_mod=seq,
            col_blocks=tuple(range(8)), kinds=x_kinds, q_scales=q_scales, rope_tabs=rope_tabs,
            name="in_projection").reshape(b, seq, 8 * group)
        ctxp = _in_projection(
            ctx2, norm_g[layer], scale_c, shift_c, wi, tm=b * n_ctx, tn=group,
            rows_per_mod=b * n_ctx, col_blocks=ctx_cols, kinds=("plain",) * 4, q_scales=q_scales,
            rope_tabs=None, name="ctx_projection").reshape(b, n_ctx, 4 * group)

        lam_p = jnp.stack([lam_q1[layer], lam_k1[layer], lam_q2[layer], lam_k2[layer]]
                          ).astype(jnp.float32)
        oa = _diff_attention(proj, ctxp, lam_p, subln_g[layer], lambda_init=lambda_init,
                             tq=512, tkb=512)
        ob = _neighbourhood_attention(proj, ctxp, _na_bias_table(rpb[layer], rows))

        h2 = _out_projection(
            oa.reshape(b * seq, group), ob.reshape(b * seq, group),
            w_out[layer].astype(jnp.bfloat16), h2, gate, final_g, tm=512, rows_per_mod=seq,
            final_norm=(layer == depth - 1))
    return h2.reshape(b, seq, d)
```

```python
import functools
import math

import numpy as np
import jax
import jax.numpy as jnp
from jax import lax
from jax.experimental import pallas as pl
from jax.experimental.pallas import tpu as pltpu

GRID_W = 64
DA_HEADS = 8
DA_QK_DIM = 64
DA_V_DIM = 2 * DA_QK_DIM
NA_HEADS = 8
NA_HEAD_DIM = 128
NA_WIN_ROWS = 8
NA_WIN_COLS = 16
ROPE_BASE = 10000.0
NORM_EPS = 1e-6
SUBLN_EPS = 1e-5

LANES = 128
VMEM_LIMIT_BYTES = 56 * 1024 * 1024
LOG2E = math.log2(math.e)
MASK_VALUE = -1e30

COL_QA, COL_KA, COL_VA, COL_GA, COL_QB, COL_KB, COL_VB, COL_GB = range(8)


def _silu(x):
    return x * (1.0 / (1.0 + jnp.exp(-x)))


def _mod_kernel(c_ref, w_ref, b_ref, o_ref):
    a = _silu(c_ref[...]).astype(jnp.bfloat16)
    w = w_ref[...].astype(jnp.bfloat16)
    o_ref[...] = jnp.dot(a, w, preferred_element_type=jnp.float32) + b_ref[...]


def _modulation(cc, w_mod, b_mod):
    rows, d = cc.shape
    n = w_mod.shape[1]
    tn = 768
    return pl.pallas_call(
        _mod_kernel,
        grid=(n // tn,),
        in_specs=[
            pl.BlockSpec((rows, d), lambda j: (0, 0)),
            pl.BlockSpec((d, tn), lambda j: (0, j)),
            pl.BlockSpec((1, tn), lambda j: (0, j)),
        ],
        out_specs=pl.BlockSpec((rows, tn), lambda j: (0, j)),
        out_shape=jax.ShapeDtypeStruct((rows, n), jnp.float32),
        compiler_params=pltpu.CompilerParams(
            dimension_semantics=("parallel",), vmem_limit_bytes=VMEM_LIMIT_BYTES),
        name="modulation",
    )(cc, w_mod, b_mod.reshape(1, n))


def _rope_tables(seq):
    t = np.arange(seq)
    row_pos = (t // GRID_W).astype(np.float32)
    col_pos = (t % GRID_W).astype(np.float32)
    half = DA_QK_DIM // 4
    inv_freq = (np.float32(ROPE_BASE) ** (-np.arange(half, dtype=np.float32) / np.float32(half))
                ).astype(np.float32)
    lane = np.arange(LANES)
    d = lane % DA_QK_DIM
    use_col = (d // (DA_QK_DIM // 2)) == 1
    f = d % half
    first_half = (d % (2 * half)) < half
    pos = np.where(use_col[None, :], col_pos[:, None], row_pos[:, None])
    ang = (pos * inv_freq[f][None, :]).astype(np.float32)
    cos = np.cos(ang.astype(np.float64)).astype(np.float32)
    sin = np.sin(ang.astype(np.float64)).astype(np.float32)
    sin_signed = np.where(first_half[None, :], -sin, sin)
    return jnp.asarray(cos), jnp.asarray(sin_signed)


def _proj_kernel(*refs, kinds, q_scales, with_rope):
    if with_rope:
        x_ref, g_ref, scale_ref, shift_ref, w_ref, cos_ref, sin_ref, o_ref, hx_ref = refs
    else:
        x_ref, g_ref, scale_ref, shift_ref, w_ref, o_ref, hx_ref = refs
    j = pl.program_id(1)

    def normed_input():
        x = x_ref[...]
        ms = jnp.mean(x * x, axis=-1, keepdims=True)
        y = x * lax.rsqrt(ms + NORM_EPS) * g_ref[...]
        hx = (y * (1.0 + scale_ref[0]) + shift_ref[0]).astype(jnp.bfloat16)
        hx_ref[...] = hx
        return hx

    def rope_store(acc, s):
        a = cos_ref[...]
        b = sin_ref[...]
        if s != 1.0:
            a = a * s
            b = b * s
        lane = lax.broadcasted_iota(jnp.int32, a.shape, 1)
        half = DA_QK_DIM // 4
        first_half = (lane % (2 * half)) < half
        for hh in range(acc.shape[1] // LANES):
            xs = acc[:, hh * LANES:(hh + 1) * LANES]
            fwd = pltpu.roll(xs, LANES - half, 1)
            bwd = pltpu.roll(xs, half, 1)
            y = xs * a + jnp.where(first_half, fwd, bwd) * b
            o_ref[:, hh * LANES:(hh + 1) * LANES] = y.astype(o_ref.dtype)

    def epilogue(kind, acc):
        if kind == "rope_q":
            rope_store(acc, q_scales[0])
        elif kind == "rope_k":
            rope_store(acc, 1.0)
        elif kind == "silu":
            o_ref[...] = _silu(acc).astype(o_ref.dtype)
        elif kind == "scale_qb":
            o_ref[...] = (acc * q_scales[1]).astype(o_ref.dtype)
        else:
            o_ref[...] = acc.astype(o_ref.dtype)

    @pl.when(j == 0)
    def _():
        hx = normed_input()
        epilogue(kinds[0], jnp.dot(hx, w_ref[...], preferred_element_type=jnp.float32))

    for kind in sorted(set(kinds[1:])):
        cols = [jj for jj in range(1, len(kinds)) if kinds[jj] == kind]
        cond = functools.reduce(jnp.logical_or, [j == jj for jj in cols])

        @pl.when(cond)
        def _(kind=kind):
            epilogue(kind, jnp.dot(hx_ref[...], w_ref[...], preferred_element_type=jnp.float32))


def _in_projection(x2, norm_g, scale, shift, w_bf16, *, tm, tn, rows_per_mod, col_blocks,
                   kinds, q_scales, rope_tabs, name):
    m, d = x2.shape
    nj = len(col_blocks)
    with_rope = rope_tabs is not None
    blocks_per_mod = rows_per_mod // tm
    start, skip = col_blocks[0], 0
    if nj > 2:
        skip = col_blocks[2] - col_blocks[1] - 1
    assert all(col_blocks[jj] == start + jj + skip * (jj // 2) for jj in range(nj))

    def w_map(i, j):
        return (0, start + j + skip * (j // 2))

    in_specs = [
        pl.BlockSpec((tm, d), lambda i, j: (i, 0)),
        pl.BlockSpec((1, d), lambda i, j: (0, 0)),
        pl.BlockSpec((1, 1, d), lambda i, j: (i // blocks_per_mod, 0, 0)),
        pl.BlockSpec((1, 1, d), lambda i, j: (i // blocks_per_mod, 0, 0)),
        pl.BlockSpec((d, tn), w_map),
    ]
    args = [x2, norm_g.reshape(1, d), scale, shift, w_bf16]
    if with_rope:
        seq_blocks = rope_tabs[0].shape[0] // tm
        in_specs += [pl.BlockSpec((tm, LANES), lambda i, j: (i % seq_blocks, 0))] * 2
        args += list(rope_tabs)
    kernel = functools.partial(_proj_kernel, kinds=tuple(kinds), q_scales=q_scales,
                               with_rope=with_rope)
    return pl.pallas_call(
        kernel,
        grid=(m // tm, nj),
        in_specs=in_specs,
        out_specs=pl.BlockSpec((tm, tn), lambda i, j: (i, j)),
        out_shape=jax.ShapeDtypeStruct((m, nj * tn), jnp.bfloat16),
        scratch_shapes=[pltpu.VMEM((tm, d), jnp.bfloat16)],
        compiler_params=pltpu.CompilerParams(
            dimension_semantics=("parallel", "arbitrary"), vmem_limit_bytes=VMEM_LIMIT_BYTES),
        name=name,
    )(*args)


def _diff_attn_kernel(q_ref, k_ref, v_ref, kc_ref, vc_ref, sg_ref, lam_ref, subg_ref, o_ref,
                      s_ref, cm_ref, m_ref, acc_ref, vt_ref, vct_ref, *, tkb, lambda_init):
    seq = k_ref.shape[1]
    tq = s_ref.shape[3]
    nkv = seq // tkb
    nq = seq // tq
    dims = (((1,), (1,)), ((), ()))

    dv = v_ref.shape[2]
    for c in range(nkv):
        vt_ref[:dv, c * tkb:(c + 1) * tkb] = v_ref[0, c * tkb:(c + 1) * tkb, :].T
    vt_ref[dv:, :] = jnp.ones((vt_ref.shape[0] - dv, seq), vt_ref.dtype)
    vct_ref[:dv, :] = vc_ref[0].T
    vct_ref[dv:, :] = jnp.ones((vct_ref.shape[0] - dv, vct_ref.shape[1]), vct_ref.dtype)

    lam_p = lam_ref[...]
    lam = (jnp.exp(jnp.sum(lam_p[0:1] * lam_p[1:2], axis=1, keepdims=True))
           - jnp.exp(jnp.sum(lam_p[2:3] * lam_p[3:4], axis=1, keepdims=True)) + lambda_init)

    def masked_q(qi):
        q = q_ref[0, pl.ds(pl.multiple_of(qi * tq, tq), tq), :]
        lane = lax.broadcasted_iota(jnp.int32, q.shape, 1)
        zero = jnp.zeros_like(q)
        return (jnp.where(lane < DA_QK_DIM, q, zero), jnp.where(lane >= DA_QK_DIM, q, zero))

    def scores_t(kblk, qmi):
        return lax.dot_general(kblk, qmi, dims, preferred_element_type=jnp.float32)

    def scores(slot, jb, qm):
        off = pl.multiple_of(jb * tkb, tkb)
        kblk = k_ref[0, pl.ds(off, tkb), :]
        for i in range(2):
            st = scores_t(kblk, qm[i])
            s_ref[slot, i] = st
            cm_ref[slot, i] = jnp.max(st, axis=0, keepdims=True)

    def online_update(i, st, cm, vt_blk):
        m_prev = m_ref[i]
        m_new = jnp.maximum(m_prev, cm)
        alpha = jnp.exp2(m_prev - m_new)
        p = jnp.exp2(st - m_new)
        acc_ref[i] = alpha * acc_ref[i] + jnp.dot(
            vt_blk, p.astype(jnp.bfloat16), preferred_element_type=jnp.float32)
        m_ref[i] = m_new

    def consume(slot, jb):
        off = pl.multiple_of(jb * tkb, tkb)
        vt_blk = vt_ref[:, pl.ds(off, tkb)]
        for i in range(2):
            online_update(i, s_ref[slot, i], cm_ref[slot, i], vt_blk)

    scores(0, 0, masked_q(0))

    def q_block(qi, carry):
        qm = masked_q(qi)
        m_ref[...] = jnp.full(m_ref.shape, -jnp.inf, jnp.float32)
        acc_ref[...] = jnp.zeros(acc_ref.shape, jnp.float32)

        for jb in range(nkv - 1):
            scores((jb + 1) % 2, jb + 1, qm)
            consume(jb % 2, jb)
        kctx = kc_ref[0]
        s_ctx = [scores_t(kctx, qm[i]) for i in range(2)]
        scores(0, 0, masked_q(jnp.minimum(qi + 1, nq - 1)))
        consume((nkv - 1) % 2, nkv - 1)
        for i in range(2):
            online_update(i, s_ctx[i], jnp.max(s_ctx[i], axis=0, keepdims=True), vct_ref[...])

        attn_t = (acc_ref[0, :dv] * (1.0 / acc_ref[0, dv:dv + 1])
                  - lam * (acc_ref[1, :dv] * (1.0 / acc_ref[1, dv:dv + 1])))
        attn = attn_t.T
        ms = jnp.mean(attn * attn, axis=-1, keepdims=True)
        y = attn * lax.rsqrt(ms + SUBLN_EPS) * subg_ref[...]
        y = y * (1.0 - lambda_init)
        rows = pl.ds(pl.multiple_of(qi * tq, tq), tq)
        o_ref[0, rows, :] = (y * sg_ref[0, rows, :].astype(jnp.float32)).astype(o_ref.dtype)
        return carry

    lax.fori_loop(0, nq, q_block, 0)


def _diff_attention(proj, ctxp, lam_p, subln_g, *, lambda_init, tq, tkb):
    b, seq, _ = proj.shape
    n_ctx = ctxp.shape[1]
    hb = DA_HEADS
    assert seq % (2 * tkb) == 0 and seq // tkb >= 2
    ones_rows = 16
    kernel = functools.partial(_diff_attn_kernel, tkb=tkb, lambda_init=lambda_init)
    return pl.pallas_call(
        kernel,
        grid=(b, DA_HEADS),
        in_specs=[
            pl.BlockSpec((1, seq, LANES), lambda bi, h: (bi, 0, COL_QA * hb + h)),
            pl.BlockSpec((1, seq, LANES), lambda bi, h: (bi, 0, COL_KA * hb + h)),
            pl.BlockSpec((1, seq, LANES), lambda bi, h: (bi, 0, COL_VA * hb + h)),
            pl.BlockSpec((1, n_ctx, LANES), lambda bi, h: (bi, 0, 0 * hb + h)),
            pl.BlockSpec((1, n_ctx, LANES), lambda bi, h: (bi, 0, 1 * hb + h)),
            pl.BlockSpec((1, seq, LANES), lambda bi, h: (bi, 0, COL_GA * hb + h)),
            pl.BlockSpec(lam_p.shape, lambda bi, h: (0, 0)),
            pl.BlockSpec((1, LANES), lambda bi, h: (0, 0)),
        ],
        out_specs=pl.BlockSpec((1, seq, LANES), lambda bi, h: (bi, 0, h)),
        out_shape=jax.ShapeDtypeStruct((b, seq, DA_HEADS * DA_V_DIM), jnp.bfloat16),
        scratch_shapes=[
            pltpu.VMEM((2, 2, tkb, tq), jnp.float32),
            pltpu.VMEM((2, 2, 1, tq), jnp.float32),
            pltpu.VMEM((2, 1, tq), jnp.float32),
            pltpu.VMEM((2, DA_V_DIM + ones_rows, tq), jnp.float32),
            pltpu.VMEM((DA_V_DIM + ones_rows, seq), jnp.bfloat16),
            pltpu.VMEM((DA_V_DIM + ones_rows, n_ctx), jnp.bfloat16),
        ],
        compiler_params=pltpu.CompilerParams(
            dimension_semantics=("parallel", "parallel"), vmem_limit_bytes=VMEM_LIMIT_BYTES),
        name="diff_attention",
    )(proj, proj, proj, ctxp, ctxp, proj, lam_p, subln_g.reshape(1, LANES))


NA_BLOCK_ROWS = NA_WIN_ROWS // 2
NA_BAND_ROWS = 3 * NA_BLOCK_ROWS


def _na_bias_table(rpb_l, rows):
    r_blk, r_band = NA_BLOCK_ROWS, NA_BAND_ROWS
    n_blocks = rows // r_blk
    row_off = np.zeros((3, r_blk, r_band), np.int32)
    row_ok = np.zeros((3, r_blk, r_band), bool)
    for variant, rb in enumerate((0, 1, n_blocks - 1)):
        band0 = min(max(rb - 1, 0), n_blocks - 3) * r_blk
        for qr in range(r_blk):
            r = rb * r_blk + qr
            r0 = min(max(r - NA_WIN_ROWS // 2, 0), rows - NA_WIN_ROWS)
            for kr in range(r_band):
                ka = band0 + kr
                row_ok[variant, qr, kr] = r0 <= ka < r0 + NA_WIN_ROWS
                row_off[variant, qr, kr] = np.clip(ka - r + NA_WIN_ROWS - 1, 0, 2 * NA_WIN_ROWS - 2)
    qc = np.arange(GRID_W)
    c0 = np.clip(qc - NA_WIN_COLS // 2, 0, GRID_W - NA_WIN_COLS)
    kc = np.arange(GRID_W)
    col_ok = (kc[None, :] >= c0[:, None]) & (kc[None, :] < c0[:, None] + NA_WIN_COLS)
    heads, n_ro, _ = rpb_l.shape
    pad = GRID_W - NA_WIN_COLS
    rp = jnp.pad(rpb_l.astype(jnp.float32) * LOG2E, ((0, 0), (0, 0), (pad, pad)))
    tiles = jnp.stack([rp[:, :, GRID_W - 1 - q:2 * GRID_W - 1 - q] for q in range(GRID_W)],
                      axis=2)
    tiles = jnp.where(jnp.asarray(col_ok)[None, None], tiles, MASK_VALUE)
    tiles_t = jnp.swapaxes(tiles, -1, -2)
    masked = jnp.full((heads, GRID_W, GRID_W), MASK_VALUE, jnp.float32)
    variants = []
    for variant in range(3):
        block_cols = []
        for qr in range(r_blk):
            block_cols.append(jnp.concatenate(
                [tiles_t[:, row_off[variant, qr, kr]] if row_ok[variant, qr, kr] else masked
                 for kr in range(r_band)], axis=-2))
        variants.append(jnp.concatenate(block_cols, axis=-1))
    return jnp.stack(variants, axis=1)


def _na_kernel(q_ref, k_ref, v_ref, kc_ref, vc_ref, sg_ref, bias_ref, o_ref,
               s_ref, v1_ref, vc1_ref):
    seq = q_ref.shape[1]
    dv = v_ref.shape[2]
    n_ctx = kc_ref.shape[1]
    tq = NA_BLOCK_ROWS * GRID_W
    band = NA_BAND_ROWS * GRID_W
    n_blocks = seq // tq
    kctx = kc_ref[0]
    dims = (((1,), (1,)), ((), ()))

    chunk = 4 * tq
    for c in range(seq // chunk):
        v1_ref[:dv, c * chunk:(c + 1) * chunk] = v_ref[0, c * chunk:(c + 1) * chunk, :].T
    v1_ref[dv:, :] = jnp.ones((v1_ref.shape[0] - dv, seq), v1_ref.dtype)
    vc1_ref[:dv, :] = vc_ref[0].T
    vc1_ref[dv:, :] = jnp.ones((vc1_ref.shape[0] - dv, n_ctx), vc1_ref.dtype)

    def band_start(rb):
        return pl.multiple_of(jnp.clip(rb - 1, 0, n_blocks - 3) * tq, tq)

    def scores(slot, rb):
        variant = jnp.where(rb == 0, 0, jnp.where(rb == n_blocks - 1, 2, 1))
        q = q_ref[0, pl.ds(pl.multiple_of(rb * tq, tq), tq), :]
        s_ref[slot, :band, :] = lax.dot_general(
            k_ref[0, pl.ds(band_start(rb), band), :], q, dims,
            preferred_element_type=jnp.float32) + bias_ref[0, variant]
        s_ref[slot, band:, :] = lax.dot_general(kctx, q, dims,
                                                preferred_element_type=jnp.float32)

    def consume(slot, rb):
        s = s_ref[slot]
        p = jnp.exp2(s - jnp.max(s, axis=0, keepdims=True)).astype(jnp.bfloat16)
        o_t = (jnp.dot(v1_ref[:, pl.ds(band_start(rb), band)], p[:band],
                       preferred_element_type=jnp.float32)
               + jnp.dot(vc1_ref[...], p[band:], preferred_element_type=jnp.float32))
        o = (o_t[:dv] * (1.0 / o_t[dv:dv + 1])).T
        rows = pl.ds(pl.multiple_of(rb * tq, tq), tq)
        o_ref[0, rows, :] = (o * sg_ref[0, rows, :].astype(jnp.float32)).astype(o_ref.dtype)

    scores(0, 0)

    def body(rr, carry):
        rb = 2 * rr
        scores(1, rb + 1)
        consume(0, rb)
        scores(0, jnp.minimum(rb + 2, n_blocks - 1))
        consume(1, rb + 1)
        return carry

    lax.fori_loop(0, n_blocks // 2, body, 0, unroll=8)


def _neighbourhood_attention(proj, ctxp, bias):
    b, seq, _ = proj.shape
    n_ctx = ctxp.shape[1]
    hb = NA_HEADS
    tq = NA_BLOCK_ROWS * GRID_W
    band = NA_BAND_ROWS * GRID_W
    ones_rows = 16
    return pl.pallas_call(
        _na_kernel,
        grid=(b, NA_HEADS),
        in_specs=[
            pl.BlockSpec((1, seq, LANES), lambda bi, h: (bi, 0, COL_QB * hb + h)),
            pl.BlockSpec((1, seq, LANES), lambda bi, h: (bi, 0, COL_KB * hb + h)),
            pl.BlockSpec((1, seq, LANES), lambda bi, h: (bi, 0, COL_VB * hb + h)),
            pl.BlockSpec((1, n_ctx, LANES), lambda bi, h: (bi, 0, 2 * hb + h)),
            pl.BlockSpec((1, n_ctx, LANES), lambda bi, h: (bi, 0, 3 * hb + h)),
            pl.BlockSpec((1, seq, LANES), lambda bi, h: (bi, 0, COL_GB * hb + h)),
            pl.BlockSpec((1, 3, band, tq), lambda bi, h: (h, 0, 0, 0)),
        ],
        out_specs=pl.BlockSpec((1, seq, LANES), lambda bi, h: (bi, 0, h)),
        out_shape=jax.ShapeDtypeStruct((b, seq, NA_HEADS * NA_HEAD_DIM), jnp.bfloat16),
        scratch_shapes=[
            pltpu.VMEM((2, band + n_ctx, tq), jnp.float32),
            pltpu.VMEM((NA_HEAD_DIM + ones_rows, seq), jnp.bfloat16),
            pltpu.VMEM((NA_HEAD_DIM + ones_rows, n_ctx), jnp.bfloat16),
        ],
        compiler_params=pltpu.CompilerParams(
            dimension_semantics=("parallel", "parallel"), vmem_limit_bytes=VMEM_LIMIT_BYTES),
        name="neighbourhood_attention",
    )(proj, proj, proj, ctxp, ctxp, proj, bias)


def _out_kernel(oa_ref, ob_ref, wa_ref, wb_ref, x_ref, gate_ref, fg_ref, o_ref, *, final_norm):
    half = oa_ref.shape[0] // 2
    for r in range(2):
        rows = slice(r * half, (r + 1) * half)
        mixed = (jnp.dot(oa_ref[rows, :], wa_ref[...], preferred_element_type=jnp.float32)
                 + jnp.dot(ob_ref[rows, :], wb_ref[...], preferred_element_type=jnp.float32))
        h = x_ref[rows, :] + gate_ref[0] * mixed
        if final_norm:
            ms = jnp.mean(h * h, axis=-1, keepdims=True)
            h = h * lax.rsqrt(ms + NORM_EPS) * fg_ref[...]
        o_ref[rows, :] = h


def _out_projection(oa2, ob2, w_out_bf16, x2, gate, final_g, *, tm, rows_per_mod, final_norm):
    m, d = x2.shape
    wa_rows = oa2.shape[1]
    wb_rows = ob2.shape[1]
    assert wa_rows == wb_rows
    blocks_per_mod = rows_per_mod // tm
    kernel = functools.partial(_out_kernel, final_norm=final_norm)
    return pl.pallas_call(
        kernel,
        grid=(m // tm,),
        in_specs=[
            pl.BlockSpec((tm, wa_rows), lambda i: (i, 0)),
            pl.BlockSpec((tm, wb_rows), lambda i: (i, 0)),
            pl.BlockSpec((wa_rows, d), lambda i: (0, 0)),
            pl.BlockSpec((wb_rows, d), lambda i: (1, 0)),
            pl.BlockSpec((tm, d), lambda i: (i, 0)),
            pl.BlockSpec((1, 1, d), lambda i: (i // blocks_per_mod, 0, 0)),
            pl.BlockSpec((1, d), lambda i: (0, 0)),
        ],
        out_specs=pl.BlockSpec((tm, d), lambda i: (i, 0)),
        out_shape=jax.ShapeDtypeStruct((m, d), jnp.float32),
        compiler_params=pltpu.CompilerParams(
            dimension_semantics=("parallel",), vmem_limit_bytes=VMEM_LIMIT_BYTES),
        name="out_projection",
    )(oa2, ob2, w_out_bf16, w_out_bf16, x2, gate, final_g.reshape(1, d))


def kernel(x, c, ctx, c_ctx, norm_g, w_mod, b_mod, w_in, w_out, lam_q1, lam_k1, lam_q2, lam_k2,
           subln_g, rpb, final_g):
    b, seq, d = x.shape
    n_ctx = ctx.shape[1]
    depth = w_in.shape[0]
    rows = seq // GRID_W
    group = DA_HEADS * DA_V_DIM
    assert w_in.shape[2] == 8 * group and NA_HEADS * NA_HEAD_DIM == group
    assert DA_V_DIM == LANES and NA_HEAD_DIM == LANES

    rope_tabs = _rope_tables(seq)
    q_scales = (DA_QK_DIM ** -0.5 * LOG2E, NA_HEAD_DIM ** -0.5 * LOG2E)
    x_kinds = ("rope_q", "rope_k", "plain", "silu", "scale_qb", "plain", "plain", "silu")
    ctx_cols = (COL_KA, COL_VA, COL_KB, COL_VB)

    mod_rows = 8
    cc = jnp.zeros((mod_rows, d), jnp.float32).at[:b].set(c).at[b].set(c_ctx)
    ctx2 = ctx.reshape(b * n_ctx, d)

    h2 = x.reshape(b * seq, d)
    for layer in range(depth):
        lambda_init = 0.8 - 0.6 * math.exp(-0.3 * layer)
        mod = _modulation(cc, w_mod[layer], b_mod[layer])
        shift = mod[:b, :d].reshape(b, 1, d)
        scale = mod[:b, d:2 * d].reshape(b, 1, d)
        gate = mod[:b, 2 * d:].reshape(b, 1, d)
        shift_c = mod[b:b + 1, :d].reshape(1, 1, d)
        scale_c = mod[b:b + 1, d:2 * d].reshape(1, 1, d)

        wi = w_in[layer].astype(jnp.bfloat16)
        proj = _in_projection(
            h2, norm_g[layer], scale, shift, wi, tm=1024, tn=group, rows_per_mod=seq,
            col_blocks=tuple(range(8)), kinds=x_kinds, q_scales=q_scales, rope_tabs=rope_tabs,
            name="in_projection").reshape(b, seq, 8 * group)
        ctxp = _in_projection(
            ctx2, norm_g[layer], scale_c, shift_c, wi, tm=b * n_ctx, tn=group,
            rows_per_mod=b * n_ctx, col_blocks=ctx_cols, kinds=("plain",) * 4, q_scales=q_scales,
            rope_tabs=None, name="ctx_projection").reshape(b, n_ctx, 4 * group)

        lam_p = jnp.stack([lam_q1[layer], lam_k1[layer], lam_q2[layer], lam_k2[layer]]
                          ).astype(jnp.float32)
        oa = _diff_attention(proj, ctxp, lam_p, subln_g[layer], lambda_init=lambda_init,
                             tq=512, tkb=512)
        ob = _neighbourhood_attention(proj, ctxp, _na_bias_table(rpb[layer], rows))

        h2 = _out_projection(
            oa.reshape(b * seq, group), ob.reshape(b * seq, group),
            w_out[layer].astype(jnp.bfloat16), h2, gate, final_g, tm=512, rows_per_mod=seq,
            final_norm=(layer == depth - 1))
    return h2.reshape(b, seq, d)
```
